```python
import math
import jax, jax.numpy as jnp
from jax import lax
import numpy as np

D_MODEL = 2048
BATCH = 8
SEQ = 2048
DEPTH = 2
DEC_BATCH = 4
DEC_SEQ = 8192
PAST_LEN = 128

D_MIX = D_MODEL
LRU_WIDTH = D_MIX // 4
LRU_BLOCKS = 8
LRU_BLOCK = LRU_WIDTH // LRU_BLOCKS
CONV_WIDTH = 4
CONV_PAD_LEFT = 2
CONV_PAD_RIGHT = 1
LRU_C = 8.0
RWKV_WIDTH = D_MIX // 4
RWKV_HEAD = 64
RWKV_HEADS = RWKV_WIDTH // RWKV_HEAD
DECAY_RANK = 64
ICLR_RANK = 64
GATE_RANK = 128
DECAY_SCALE = 0.606531
GN_EPS = 64e-5
ATTN_WIDTH = D_MIX - LRU_WIDTH - RWKV_WIDTH
HEAD_DIM = 128
N_Q_HEADS = ATTN_WIDTH // HEAD_DIM
N_KV_HEADS = 2
GQA_GROUP = N_Q_HEADS // N_KV_HEADS
KV_WIDTH = N_KV_HEADS * HEAD_DIM
WINDOW = 128
BLOCK = 128
N_BUCKETS = 32
MAX_DISTANCE = 128
D_FF = 4 * D_MODEL
EPS = 1e-6
OFF_LRU_X = 0
OFF_LRU_G = LRU_WIDTH
OFF_RWKV = 2 * LRU_WIDTH
RWKV_COLS = 3 * RWKV_WIDTH + 2 * DECAY_RANK + 2 * ICLR_RANK + GATE_RANK
OFF_ATTN = OFF_RWKV + RWKV_COLS
ATTN_COLS = ATTN_WIDTH + 2 * KV_WIDTH
D_IN = OFF_ATTN + ATTN_COLS

kernel_name = 'hymba_lru_rwkv7_swa_encoder'


def _rms_norm(x, g):
    xf = x.astype(jnp.float32)
    y = xf * lax.rsqrt(jnp.mean(xf * xf, axis=-1, keepdims=True) + EPS)
    return (y * g.astype(jnp.float32)).astype(x.dtype)


def _lin_combine(left, right):
    a_l, b_l = left
    a_r, b_r = right
    return a_l * a_r, a_r * b_l + b_r


def _rg_lru(xb, gb, conv_w, conv_b, wa, ba, wx, bx, lam):
    f32 = jnp.float32
    bsz, slen, _ = xb.shape
    xf = xb.astype(f32)
    xc = lax.conv_general_dilated(xf, conv_w.astype(f32)[:, None, :], (1,), [(CONV_PAD_LEFT, CONV_PAD_RIGHT)],
                                  dimension_numbers=('NWC', 'WIO', 'NWC'),
                                  feature_group_count=LRU_WIDTH) + conv_b.astype(f32)
    xh = xc.reshape(bsz, slen, LRU_BLOCKS, LRU_BLOCK)
    r = jax.nn.sigmoid(jnp.einsum('bsni,dnij->dbsnj', xh, wa.astype(f32)).reshape(2, bsz, slen, LRU_WIDTH)
                       + ba.astype(f32)[:, None, None, :])
    i = jax.nn.sigmoid(jnp.einsum('bsni,dnij->dbsnj', xh, wx.astype(f32)).reshape(2, bsz, slen, LRU_WIDTH)
                       + bx.astype(f32)[:, None, None, :])
    log_a = -LRU_C * r * jax.nn.softplus(-lam.astype(f32))[:, None, None, :]
    a = jnp.exp(log_a)
    mult = jnp.sqrt(-jnp.expm1(2.0 * log_a))
    pos = jnp.arange(slen)
    first = jnp.stack([pos == 0, pos == slen - 1])[:, None, :, None]
    mult = jnp.where(first, 1.0, mult)
    u = mult * i * xc[None]
    h_f = lax.associative_scan(_lin_combine, (a[0], u[0]), axis=1)[1]
    h_b = lax.associative_scan(_lin_combine, (a[1], u[1]), axis=1, reverse=True)[1]
    return (h_f + h_b) * jax.nn.gelu(gb.astype(f32), approximate=True)


def _centred_shift(z):
    zp = jnp.pad(z, ((0, 0), (1, 1), (0, 0)))
    return 0.5 * (zp[:, :-2] + zp[:, 2:])


def _heads(t):
    return t.reshape(t.shape[:-1] + (RWKV_HEADS, RWKV_HEAD))


def _rwkv_scan(r, w, kk, a, k, v, reverse):
    bsz, _, nh, n = r.shape
    xs = tuple(jnp.moveaxis(t, 1, 0) for t in (r, w, kk, a, k, v))

    def step(st, inp):
        r_t, w_t, kk_t, a_t, k_t, v_t = inp
        sa = jnp.einsum('bhvk,bhk->bhv', st, kk_t)
        st = (st * w_t[:, :, None, :] - sa[..., None] * (kk_t * a_t)[:, :, None, :]
              + v_t[..., None] * k_t[:, :, None, :])
        y = jnp.einsum('bhvk,bhk->bhv', st, r_t)
        return st, y

    s0 = jnp.zeros((bsz, nh, n, n), jnp.float32)
    _, ys = lax.scan(step, s0, xs, reverse=reverse)
    return jnp.moveaxis(ys, 0, 1)


def _rwkv7(z, mu, w0, w2, a0, a2, g2, k_k, k_a, r_k, gn_w, gn_b):
    f32 = jnp.float32
    bsz, slen, _ = z.shape
    W = RWKV_WIDTH
    z = z.astype(f32)
    z = z + (_centred_shift(z) - z) * mu.astype(f32)
    r = z[..., 0:W]
    k = z[..., W:2 * W]
    v = z[..., 2 * W:3 * W]
    o = 3 * W
    dw = z[..., o:o + 2 * DECAY_RANK].reshape(bsz, slen, 2, DECAY_RANK)
    o += 2 * DECAY_RANK
    da = z[..., o:o + 2 * ICLR_RANK].reshape(bsz, slen, 2, ICLR_RANK)
    o += 2 * ICLR_RANK
    dg = z[..., o:o + GATE_RANK]
    decay = jnp.exp(-DECAY_SCALE * jax.nn.sigmoid(
        jnp.einsum('bsdr,drc->dbsc', jnp.tanh(dw), w2.astype(f32)) + w0.astype(f32)[:, None, None, :]))
    a = jax.nn.sigmoid(jnp.einsum('bsdr,drc->dbsc', da, a2.astype(f32)) + a0.astype(f32)[:, None, None, :])
    g = jax.nn.sigmoid(dg) @ g2.astype(f32)
    kdir = k[None] * (1.0 + (a - 1.0) * k_a.astype(f32)[:, None, None, :])
    kk = _heads(k * k_k.astype(f32))
    kk = kk * lax.rsqrt(jnp.maximum(jnp.sum(kk * kk, axis=-1, keepdims=True), 1e-24))
    rh, vh = _heads(r), _heads(v)
    decay_h, a_h, kdir_h = _heads(decay), _heads(a), _heads(kdir)
    y = (_rwkv_scan(rh, decay_h[0], kk, a_h[0], kdir_h[0], vh, False)
         + _rwkv_scan(rh, decay_h[1], kk, a_h[1], kdir_h[1], vh, True))
    mean = jnp.mean(y, axis=-1, keepdims=True)
    var = jnp.mean(jnp.square(y - mean), axis=-1, keepdims=True)
    yn = ((y - mean) * lax.rsqrt(var + GN_EPS)).reshape(bsz, slen, W) * gn_w.astype(f32) + gn_b.astype(f32)
    bonus = jnp.sum(rh * (kdir_h[0] + kdir_h[1]) * r_k.astype(f32), axis=-1, keepdims=True) * vh
    return (yn + bonus.reshape(bsz, slen, W)) * g


def _t5_bucket(rel):
    half = N_BUCKETS // 2
    max_exact = half // 2
    ret = jnp.where(rel > 0, half, 0)
    n = jnp.abs(rel)
    nf = jnp.maximum(n, 1).astype(jnp.float32)
    large = max_exact + (jnp.log(nf / max_exact) / math.log(MAX_DISTANCE / max_exact)
                         * (half - max_exact)).astype(jnp.int32)
    large = jnp.minimum(large, half - 1)
    return ret + jnp.where(n < max_exact, n, large)


def _window_attention(q, k, v, sink, rel_bias):
    f32 = jnp.float32
    bsz, slen = q.shape[:2]
    nb = slen // BLOCK
    qb = q.reshape(bsz, nb, BLOCK, N_KV_HEADS, GQA_GROUP, HEAD_DIM)

    def band(t):
        tp = jnp.pad(t, ((0, 0), (BLOCK, BLOCK), (0, 0), (0, 0))).reshape(bsz, nb + 2, BLOCK, N_KV_HEADS, HEAD_DIM)
        return jnp.concatenate([tp[:, :-2], tp[:, 1:-1], tp[:, 2:]], axis=2)

    kb, vb = band(k), band(v)
    s = jnp.einsum('bnqhgd,bnkhd->bnhgqk', qb, kb).astype(f32) * (HEAD_DIM ** -0.5)
    qi = jnp.arange(BLOCK)[:, None]
    kj = jnp.arange(3 * BLOCK)[None, :]
    rel = kj - BLOCK - qi
    bias = rel_bias.astype(f32)[_t5_bucket(rel)]
    bias = jnp.transpose(bias, (2, 0, 1)).reshape(N_KV_HEADS, GQA_GROUP, BLOCK, 3 * BLOCK)
    kpos = jnp.arange(nb)[:, None] * BLOCK + kj - BLOCK
    valid = ((jnp.abs(rel) <= WINDOW)[None] & (kpos >= 0)[:, None, :] & (kpos < slen)[:, None, :])
    s = jnp.where(valid[None, :, None, None], s + bias, -jnp.inf)
    sk = sink.astype(f32).reshape(N_KV_HEADS, GQA_GROUP)[None, None, :, :, None, None]
    m = jnp.maximum(jnp.max(s, axis=-1, keepdims=True), sk)
    p = jnp.exp(s - m)
    den = jnp.sum(p, axis=-1, keepdims=True) + jnp.exp(sk - m)
    o = jnp.einsum('bnhgqk,bnkhd->bnqhgd', (p / den).astype(v.dtype), vb)
    return o.reshape(bsz, slen, ATTN_WIDTH)


def _trunk(x, norm_mix_pre, norm_mix_post, norm_ffn_pre, norm_ffn_post, w_in, w_out,
           conv_w, conv_b, lru_wa, lru_ba, lru_wx, lru_bx, lru_lambda,
           rwkv_mu, rwkv_w0, rwkv_w2, rwkv_a0, rwkv_a2, rwkv_g2, rwkv_k_k, rwkv_k_a, rwkv_r_k,
           rwkv_gn_w, rwkv_gn_b, attn_sink, rel_bias, w_up, w_down):
    bsz, slen, _ = x.shape
    for l in range(DEPTH):
        h = _rms_norm(x, norm_mix_pre[l])
        p = h @ w_in[l]
        lru_out = _rg_lru(p[..., OFF_LRU_X:OFF_LRU_G], p[..., OFF_LRU_G:OFF_RWKV], conv_w[l], conv_b[l],
                          lru_wa[l], lru_ba[l], lru_wx[l], lru_bx[l], lru_lambda[l])
        rwkv_out = _rwkv7(p[..., OFF_RWKV:OFF_ATTN], rwkv_mu[l], rwkv_w0[l], rwkv_w2[l], rwkv_a0[l], rwkv_a2[l],
                          rwkv_g2[l], rwkv_k_k[l], rwkv_k_a[l], rwkv_r_k[l], rwkv_gn_w[l], rwkv_gn_b[l])
        q = p[..., OFF_ATTN:OFF_ATTN + ATTN_WIDTH].reshape(bsz, slen, N_Q_HEADS, HEAD_DIM)
        k = p[..., OFF_ATTN + ATTN_WIDTH:OFF_ATTN + ATTN_WIDTH + KV_WIDTH].reshape(bsz, slen, N_KV_HEADS, HEAD_DIM)
        v = p[..., OFF_ATTN + ATTN_WIDTH + KV_WIDTH:D_IN].reshape(bsz, slen, N_KV_HEADS, HEAD_DIM)
        attn_out = _window_attention(q, k, v, attn_sink[l], rel_bias)
        mix = jnp.concatenate([lru_out, rwkv_out, attn_out.astype(jnp.float32)], axis=-1).astype(x.dtype) @ w_out[l]
        x = x + _rms_norm(mix, norm_mix_post[l])
        h = _rms_norm(x, norm_ffn_pre[l])
        f = jnp.square(jax.nn.relu(h @ w_up[l])) @ w_down[l]
        x = x + _rms_norm(f, norm_ffn_post[l])
    return x


def setup_inputs(seed: int = 0) -> dict:
    key = jax.random.key(seed)
    ks = jax.random.split(key, 32)
    f32 = jnp.float32
    L = DEPTH

    def nrm(k, shape, scale):
        return jax.random.normal(k, shape, f32) * scale

    def gain(k, shape):
        return 1.0 + 0.05 * jax.random.normal(k, shape, f32)

    u = jax.random.uniform(ks[14], (L, 2, LRU_WIDTH), f32, 0.9, 0.999)
    a_base = u ** (1.0 / LRU_C)
    lam = jnp.log(a_base) - jnp.log1p(-a_base)
    return {
        'x_prompt': nrm(ks[0], (BATCH, SEQ, D_MODEL), 1.0),
        'x_sample': nrm(ks[1], (DEC_BATCH, DEC_SEQ, D_MODEL), 1.0),
        'norm_mix_pre': gain(ks[2], (L, D_MODEL)),
        'norm_mix_post': gain(ks[3], (L, D_MODEL)),
        'norm_ffn_pre': gain(ks[4], (L, D_MODEL)),
        'norm_ffn_post': gain(ks[5], (L, D_MODEL)),
        'w_in': nrm(ks[6], (L, D_MODEL, D_IN), D_MODEL ** -0.5),
        'w_out': nrm(ks[7], (L, D_MIX, D_MODEL), D_MIX ** -0.5),
        'conv_w': nrm(ks[8], (L, CONV_WIDTH, LRU_WIDTH), CONV_WIDTH ** -0.5),
        'conv_b': nrm(ks[9], (L, LRU_WIDTH), 0.01),
        'lru_wa': nrm(ks[10], (L, 2, LRU_BLOCKS, LRU_BLOCK, LRU_BLOCK), LRU_BLOCK ** -0.5),
        'lru_ba': nrm(ks[11], (L, 2, LRU_WIDTH), 0.01),
        'lru_wx': nrm(ks[12], (L, 2, LRU_BLOCKS, LRU_BLOCK, LRU_BLOCK), LRU_BLOCK ** -0.5),
        'lru_bx': nrm(ks[13], (L, 2, LRU_WIDTH), 0.01),
        'lru_lambda': lam,
        'rwkv_mu': jax.random.uniform(ks[15], (L, RWKV_COLS), f32, 0.0, 1.0),
        'rwkv_w0': jax.random.uniform(ks[16], (L, 2, RWKV_WIDTH), f32, -2.0, 2.0),
        'rwkv_w2': nrm(ks[17], (L, 2, DECAY_RANK, RWKV_WIDTH), 0.5 * DECAY_RANK ** -0.5),
        'rwkv_a0': nrm(ks[18], (L, 2, RWKV_WIDTH), 0.5),
        'rwkv_a2': nrm(ks[19], (L, 2, ICLR_RANK, RWKV_WIDTH), 0.5 * ICLR_RANK ** -0.5),
        'rwkv_g2': nrm(ks[20], (L, GATE_RANK, RWKV_WIDTH), GATE_RANK ** -0.5),
        'rwkv_k_k': 0.85 + 0.05 * jax.random.normal(ks[21], (L, RWKV_WIDTH), f32),
        'rwkv_k_a': gain(ks[22], (L, 2, RWKV_WIDTH)),
        'rwkv_r_k': nrm(ks[23], (L, RWKV_HEADS, RWKV_HEAD), 0.1),
        'rwkv_gn_w': gain(ks[24], (L, RWKV_WIDTH)),
        'rwkv_gn_b': nrm(ks[25], (L, RWKV_WIDTH), 0.01),
        'attn_sink': nrm(ks[26], (L, N_Q_HEADS), 0.5),
        'rel_bias': nrm(ks[27], (N_BUCKETS, N_Q_HEADS), 0.5),
        'w_up': nrm(ks[28], (L, D_MODEL, D_FF), D_MODEL ** -0.5),
        'w_down': nrm(ks[29], (L, D_FF, D_MODEL), D_FF ** -0.5),
    }


def reference(x_prompt, x_sample, norm_mix_pre, norm_mix_post, norm_ffn_pre, norm_ffn_post, w_in, w_out,
              conv_w, conv_b, lru_wa, lru_ba, lru_wx, lru_bx, lru_lambda,
              rwkv_mu, rwkv_w0, rwkv_w2, rwkv_a0, rwkv_a2, rwkv_g2, rwkv_k_k, rwkv_k_a, rwkv_r_k,
              rwkv_gn_w, rwkv_gn_b, attn_sink, rel_bias, w_up, w_down):
    weights = (norm_mix_pre, norm_mix_post, norm_ffn_pre, norm_ffn_post, w_in, w_out,
               conv_w, conv_b, lru_wa, lru_ba, lru_wx, lru_bx, lru_lambda,
               rwkv_mu, rwkv_w0, rwkv_w2, rwkv_a0, rwkv_a2, rwkv_g2, rwkv_k_k, rwkv_k_a, rwkv_r_k,
               rwkv_gn_w, rwkv_gn_b, attn_sink, rel_bias, w_up, w_down)
    y_prompt = _trunk(x_prompt, *weights)
    y_sample = _trunk(x_sample, *weights)
    return (y_prompt, y_sample)
```

```python
import functools
import math

import numpy as np
import jax
import jax.numpy as jnp
from jax import lax
from jax.experimental import pallas as pl
from jax.experimental.pallas import tpu as pltpu

F32 = jnp.float32
BF16 = jnp.bfloat16

D_MODEL = 2048
LRU_WIDTH = 512
LRU_BLOCKS = 8
LRU_C = 8.0
RWKV_WIDTH = 512
RWKV_HEAD = 64
RWKV_HEADS = 8
LORA_RANK = 64
GATE_RANK = 128
DECAY_SCALE = 0.606531
GN_EPS = 64e-5
ATTN_WIDTH = 1024
HEAD_DIM = 128
N_Q_HEADS = 8
N_KV_HEADS = 2
GQA_GROUP = 4
KV_WIDTH = 256
WINDOW = 128
ATTN_BLOCK = 128
N_BUCKETS = 32
MAX_DISTANCE = 128
D_FF = 8192
EPS = 1e-6
OFF_RWKV = 2 * LRU_WIDTH
RWKV_COLS = 3 * RWKV_WIDTH + 4 * LORA_RANK + GATE_RANK
OFF_ATTN = OFF_RWKV + RWKV_COLS

RWKV_PAD = 2048
P_LRU_X = 2048
P_LRU_G = 2560
P_Q = 3072
P_K = 4096
P_V = 4352
P_COLS = 4608

LANES = 128
HALF = 256
VMEM_LIMIT = 56 * 1024 * 1024

TM_PROJ = 1024
TN_PROJ = 768
TM_OUT = 512
TM_FFN = 512
TF_FFN = 1024
TB_LRU = 256
TB_RWKV = 256
CHUNK = 64
NEG = -1e30


def _dot(a, b):
    return jnp.dot(a, b, preferred_element_type=F32)


def _dot_nt(a, b):
    return lax.dot_general(a, b, (((1,), (1,)), ((), ())), preferred_element_type=F32)


def _dot_tn(a, b):
    return lax.dot_general(a, b, (((0,), (0,)), ((), ())), preferred_element_type=F32)


def _split2(x):
    hi = x.astype(BF16)
    lo = (x - hi.astype(F32)).astype(BF16)
    return hi, lo


def _split3(x):
    hi = x.astype(BF16)
    r1 = x - hi.astype(F32)
    mid = r1.astype(BF16)
    lo = (r1 - mid.astype(F32)).astype(BF16)
    return hi, mid, lo


def _params(sem):
    return pltpu.CompilerParams(dimension_semantics=sem, vmem_limit_bytes=VMEM_LIMIT)


def _inproj_kernel(x_ref, g_ref, w_ref, o_ref, h_ref):
    @pl.when(pl.program_id(1) == 0)
    def _():
        x = x_ref[...]
        ms = jnp.mean(x * x, axis=-1, keepdims=True)
        h_ref[...] = (x * lax.rsqrt(ms + EPS) * g_ref[...]).astype(BF16)

    o_ref[...] = _dot(h_ref[...], w_ref[...])


def _inproj(x2, gain, w):
    t = x2.shape[0]
    tm = min(TM_PROJ, t)
    return pl.pallas_call(
        _inproj_kernel,
        grid=(t // tm, P_COLS // TN_PROJ),
        in_specs=[
            pl.BlockSpec((tm, D_MODEL), lambda i, j: (i, 0)),
            pl.BlockSpec((1, D_MODEL), lambda i, j: (0, 0)),
            pl.BlockSpec((D_MODEL, TN_PROJ), lambda i, j: (0, j)),
        ],
        out_specs=pl.BlockSpec((tm, TN_PROJ), lambda i, j: (i, j)),
        out_shape=jax.ShapeDtypeStruct((t, P_COLS), F32),
        scratch_shapes=[pltpu.VMEM((tm, D_MODEL), BF16)],
        compiler_params=_params(("arbitrary", "arbitrary")),
        name="inproj",
    )(x2, gain, w)


def _time_block(j, nb, reverse):
    return nb - 1 - j if reverse else j


def _halo_specs(tb, nb, width, col, reverse):
    rows8 = tb // 8
    last8 = nb * rows8 - 1

    def blk(b, j):
        return (b, _time_block(j, nb, reverse), col)

    def prev(b, j):
        return (b, jnp.maximum(_time_block(j, nb, reverse) * rows8 - 1, 0), col)

    def nxt(b, j):
        return (b, jnp.minimum((_time_block(j, nb, reverse) + 1) * rows8, last8), col)

    return [pl.BlockSpec((None, tb, width), blk),
            pl.BlockSpec((None, 8, width), prev),
            pl.BlockSpec((None, 8, width), nxt)]


def _shift_rows(x, prev8, next8, row, tb):
    xm1 = jnp.where(row == 0, prev8[7:8], pltpu.roll(x, 1, 0))
    xm2 = jnp.where(row == 0, prev8[6:7], jnp.where(row == 1, prev8[7:8], pltpu.roll(x, 2, 0)))
    xp1 = jnp.where(row == tb - 1, next8[0:1], pltpu.roll(x, tb - 1, 0))
    return xm2, xm1, xp1


def _softplus(z):
    return jnp.maximum(z, 0.0) + jnp.log(1.0 + jnp.exp(-jnp.abs(z)))


def _lru_kernel(*refs, reverse, final, tb, nb, seq):
    if final:
        (x_ref, xp_ref, xn_ref, g_ref, hf_ref, cw_ref, cb_ref, wa_ref, wx_ref, ba_ref, bx_ref, lam_ref,
         o_ref, carry_ref) = refs
    else:
        (x_ref, xp_ref, xn_ref, cw_ref, cb_ref, wa_ref, wx_ref, ba_ref, bx_ref, lam_ref,
         o_ref, carry_ref) = refs
    j = pl.program_id(1)
    jj = _time_block(j, nb, reverse)

    @pl.when(j == 0)
    def _():
        carry_ref[...] = jnp.zeros_like(carry_ref)

    x = x_ref[...]
    prev8 = jnp.where(jj > 0, xp_ref[...], 0.0)
    next8 = jnp.where(jj < nb - 1, xn_ref[...], 0.0)
    row = lax.broadcasted_iota(jnp.int32, (tb, 1), 0)
    xm2, xm1, xp1 = _shift_rows(x, prev8, next8, row, tb)
    cw = cw_ref[...]
    xc = cw[0:1] * xm2 + cw[1:2] * xm1 + cw[2:3] * x + cw[3:4] * xp1 + cb_ref[...]

    xcb = xc.astype(BF16)
    r = jax.nn.sigmoid(_dot(xcb, wa_ref[...]) + ba_ref[...])
    i = jax.nn.sigmoid(_dot(xcb, wx_ref[...]) + bx_ref[...])
    log_a = -LRU_C * r * _softplus(-lam_ref[...])
    a = jnp.exp(log_a)
    mult = jnp.sqrt(1.0 - a * a)
    gpos = jj * tb + row
    first = (gpos == seq - 1) if reverse else (gpos == 0)
    mult = jnp.where(first, 1.0, mult)
    u = mult * i * xc

    acc_a, acc_u = a, u
    s = 1
    while s < tb:
        if reverse:
            sh_a, sh_u, valid = pltpu.roll(acc_a, tb - s, 0), pltpu.roll(acc_u, tb - s, 0), row < tb - s
        else:
            sh_a, sh_u, valid = pltpu.roll(acc_a, s, 0), pltpu.roll(acc_u, s, 0), row >= s
        acc_u = acc_a * jnp.where(valid, sh_u, 0.0) + acc_u
        acc_a = acc_a * jnp.where(valid, sh_a, 1.0)
        s *= 2
    h = acc_a * carry_ref[0:1, :] + acc_u
    last = h[0:1] if reverse else h[tb - 1:tb]
    carry_ref[...] = jnp.broadcast_to(last, carry_ref.shape)

    if final:
        o_ref[...] = ((hf_ref[...] + h) * jax.nn.gelu(g_ref[...], approximate=True)).astype(o_ref.dtype)
    else:
        o_ref[...] = h


def _lru(p, hf, wl, d, final):
    bsz, seq, _ = p.shape
    tb = min(TB_LRU, seq)
    nb = seq // tb
    reverse = d == 1
    in_specs = _halo_specs(tb, nb, LRU_WIDTH, P_LRU_X // LRU_WIDTH, reverse)
    args = [p, p, p]
    blk = lambda b, j: (b, _time_block(j, nb, reverse), 0)
    if final:
        in_specs += [pl.BlockSpec((None, tb, LRU_WIDTH), lambda b, j: (b, _time_block(j, nb, reverse), P_LRU_G // LRU_WIDTH)),
                     pl.BlockSpec((None, tb, LRU_WIDTH), blk)]
        args += [p, hf]
    const = lambda shape: pl.BlockSpec(shape, lambda b, j: (0,) * len(shape))
    in_specs += [const((4, LRU_WIDTH)), const((1, LRU_WIDTH)), const((LRU_WIDTH, LRU_WIDTH)),
                 const((LRU_WIDTH, LRU_WIDTH)), const((1, LRU_WIDTH)), const((1, LRU_WIDTH)), const((1, LRU_WIDTH))]
    args += [wl["conv_w"], wl["conv_b"], wl["lru_wa"][d], wl["lru_wx"][d], wl["lru_ba"][d], wl["lru_bx"][d],
             wl["lru_lambda"][d]]
    return pl.pallas_call(
        functools.partial(_lru_kernel, reverse=reverse, final=final, tb=tb, nb=nb, seq=seq),
        grid=(bsz, nb),
        in_specs=in_specs,
        out_specs=pl.BlockSpec((None, tb, LRU_WIDTH), blk),
        out_shape=jax.ShapeDtypeStruct((bsz, seq, LRU_WIDTH), BF16 if final else F32),
        scratch_shapes=[pltpu.VMEM((8, LRU_WIDTH), F32)],
        compiler_params=_params(("arbitrary", "arbitrary")),
        name="lru_bwd" if final else "lru_fwd",
    )(*args)


def _seg_sum(x, ones_bd):
    hi, lo = _split2(x)
    return _dot(hi, ones_bd) + _dot(lo, ones_bd)


def _rwkv_kernel(*refs, reverse, final, tb, nb, chunk):
    if final:
        (z_ref, zp_ref, zn_ref, yf_ref, mu_ref, w0_ref, w2_ref, a0_ref, a2_ref, kk_ref, ka_ref,
         a0f_ref, a2f_ref, kaf_ref, g2_ref, rk_ref, gnw_ref, gnb_ref,
         o_ref, s_ref, qk_s, rt_s, kt_s, bt_s, ktg_s, btg_s, v_s, egl_s, y_s) = refs
    else:
        (z_ref, zp_ref, zn_ref, mu_ref, w0_ref, w2_ref, a0_ref, a2_ref, kk_ref, ka_ref,
         o_ref, s_ref, qk_s, rt_s, kt_s, bt_s, ktg_s, btg_s, v_s, egl_s, y_s) = refs
    j = pl.program_id(1)
    jj = _time_block(j, nb, reverse)
    W = RWKV_WIDTH
    nchunk = tb // chunk
    group = HALF // chunk

    @pl.when(j == 0)
    def _():
        s_ref[...] = jnp.zeros_like(s_ref)

    z = z_ref[...]
    prev8 = jnp.where(jj > 0, zp_ref[...], 0.0)
    next8 = jnp.where(jj < nb - 1, zn_ref[...], 0.0)
    row = lax.broadcasted_iota(jnp.int32, (tb, 1), 0)
    zm1 = jnp.where(row == 0, prev8[7:8], pltpu.roll(z, 1, 0))
    zp1 = jnp.where(row == tb - 1, next8[0:1], pltpu.roll(z, tb - 1, 0))
    zs = z + (0.5 * (zm1 + zp1) - z) * mu_ref[...]
    r = zs[:, 0:W]
    k = zs[:, W:2 * W]
    v = zs[:, 2 * W:3 * W]
    dw = zs[:, 3 * W:3 * W + 2 * LORA_RANK]
    da = zs[:, 3 * W + 2 * LORA_RANK:3 * W + 4 * LORA_RANK]
    dg = zs[:, 3 * W + 4 * LORA_RANK:3 * W + 4 * LORA_RANK + GATE_RANK]

    lw = -DECAY_SCALE * jax.nn.sigmoid(_dot(jnp.tanh(dw).astype(BF16), w2_ref[...]) + w0_ref[...])
    dab = da.astype(BF16)
    a = jax.nn.sigmoid(_dot(dab, a2_ref[...]) + a0_ref[...])
    kdir = k * (1.0 + (a - 1.0) * ka_ref[...])
    ci = lax.broadcasted_iota(jnp.int32, (W, W), 0) // RWKV_HEAD
    cj = lax.broadcasted_iota(jnp.int32, (W, W), 1) // RWKV_HEAD
    ones_bd = jnp.where(ci == cj, 1.0, 0.0).astype(BF16)
    kk = k * kk_ref[...]
    kk = kk * lax.rsqrt(jnp.maximum(_seg_sum(kk * kk, ones_bd), 1e-24))
    b = kk * a

    ti = lax.broadcasted_iota(jnp.int32, (tb, tb), 0)
    tj = lax.broadcasted_iota(jnp.int32, (tb, tb), 1)
    same = (ti // chunk) == (tj // chunk)
    before = (tj >= ti) if reverse else (tj <= ti)
    tri = jnp.where(same & before, 1.0, 0.0).astype(BF16)
    tot = jnp.where(same, 1.0, 0.0).astype(BF16)
    hi, mid, lo = _split3(lw)
    g = _dot(tri, hi) + _dot(tri, mid) + _dot(tri, lo)
    gl = _dot(tot, hi) + _dot(tot, mid) + _dot(tot, lo)
    e_g = jnp.exp(g)
    e_ng = jnp.exp(-g)
    e_gl = jnp.exp(gl)
    kt = kdir * e_ng
    bt = b * e_ng
    qk_s[...] = (kk * jnp.exp(g - lw)).astype(BF16)
    rt_s[...] = (r * e_g).astype(BF16)
    kt_s[...] = kt.astype(BF16)
    bt_s[...] = bt.astype(BF16)
    ktg_s[...] = (kt * e_gl).astype(BF16)
    btg_s[...] = (-(bt * e_gl)).astype(BF16)
    v_s[...] = v.astype(BF16)
    egl_s[...] = e_gl

    hr = lax.broadcasted_iota(jnp.int32, (HALF, HALF), 0)
    hc = lax.broadcasted_iota(jnp.int32, (HALF, HALF), 1)
    head_mask = jnp.where((hr // chunk) == (hc // RWKV_HEAD), 1.0, 0.0).astype(BF16)
    bd_mask = jnp.where((hr // chunk) == (hc // chunk), 1.0, 0.0).astype(BF16)
    st_mask = (hr // RWKV_HEAD) == (hc // RWKV_HEAD)
    wi = lax.broadcasted_iota(jnp.int32, (chunk, HALF), 0)
    wj = lax.broadcasted_iota(jnp.int32, (chunk, HALF), 1) % chunk
    strict = (wj > wi) if reverse else (wj < wi)
    incl = (wj >= wi) if reverse else (wj <= wi)
    eye_w = jnp.where(wi == wj, 1.0, 0.0)

    def stack(xb):
        return jnp.concatenate([xb] * group, axis=0) * head_mask

    def blockdiag(xw):
        return jnp.concatenate([xw.astype(BF16)] * group, axis=0) * bd_mask

    n_doubling = int(math.log2(chunk)) - 1

    def chunk_body(ci_, carry):
        c = (nchunk - 1 - ci_) if reverse else ci_
        sl = pl.ds(pl.multiple_of(c * chunk, chunk), chunk)
        for hf in range(W // HALF):
            ls = slice(HALF * hf, HALF * hf + HALF)
            qk = qk_s[sl, ls]
            rt = rt_s[sl, ls]
            vb = v_s[sl, ls]
            st = s_ref[hf]
            x2 = jnp.concatenate([qk, rt], axis=0)
            xs = _dot_nt(x2, st.astype(BF16))
            km = stack(kt_s[sl, ls])
            bm = stack(bt_s[sl, ls])
            vm = stack(vb)
            gk = _dot_nt(x2, km)
            gb = _dot_nt(x2, bm)
            a_k = jnp.where(strict, gk[:chunk], 0.0)
            a_b = jnp.where(strict, gb[:chunk], 0.0)
            b_k = jnp.where(incl, gk[chunk:], 0.0)
            b_b = jnp.where(incl, gb[chunk:], 0.0)
            rhs = xs[:chunk] + _dot(a_k.astype(BF16), vm)
            pw = -a_b
            inv = eye_w + pw
            for _ in range(n_doubling):
                pw = _dot(pw.astype(BF16), blockdiag(pw))
                inv = inv + _dot(inv.astype(BF16), blockdiag(pw))
            cm = _dot(inv.astype(BF16), stack(rhs.astype(BF16)))
            cmb = cm.astype(BF16)
            y = xs[chunk:] + _dot(b_k.astype(BF16), vm) - _dot(b_b.astype(BF16), stack(cmb))
            y_s[sl, ls] = y
            kg = jnp.concatenate([ktg_s[sl, ls], btg_s[sl, ls]], axis=0)
            vc = jnp.concatenate([vb, cmb], axis=0)
            upd = _dot_tn(vc, kg)
            decay = egl_s[sl, ls][0:1]
            s_ref[hf] = jnp.where(st_mask, st * decay + upd, 0.0)
        return carry

    lax.fori_loop(0, nchunk, chunk_body, 0)

    if not final:
        o_ref[...] = y_s[...]
    else:
        y = yf_ref[...] + y_s[...]
        inv_n = 1.0 / RWKV_HEAD
        mean = _seg_sum(y, ones_bd) * inv_n
        yc = y - mean
        var = _seg_sum(yc * yc, ones_bd) * inv_n
        yn = yc * lax.rsqrt(var + GN_EPS) * gnw_ref[...] + gnb_ref[...]
        a_f = jax.nn.sigmoid(_dot(dab, a2f_ref[...]) + a0f_ref[...])
        kdir_f = k * (1.0 + (a_f - 1.0) * kaf_ref[...])
        bonus = _seg_sum(r * (kdir_f + kdir) * rk_ref[...], ones_bd) * v
        gate = _dot(jax.nn.sigmoid(dg).astype(BF16), g2_ref[...])
        o_ref[...] = ((yn + bonus) * gate).astype(o_ref.dtype)


def _rwkv(p, yf, wl, d, final):
    bsz, seq, _ = p.shape
    tb = min(TB_RWKV, seq)
    nb = seq // tb
    reverse = d == 1
    W = RWKV_WIDTH
    in_specs = _halo_specs(tb, nb, RWKV_PAD, 0, reverse)
    args = [p, p, p]
    blk = lambda b, j: (b, _time_block(j, nb, reverse), 0)
    const = lambda shape: pl.BlockSpec(shape, lambda b, j: (0,) * len(shape))
    if final:
        in_specs += [pl.BlockSpec((None, tb, W), blk)]
        args += [yf]
    in_specs += [const((1, RWKV_PAD)), const((1, W)), const((2 * LORA_RANK, W)), const((1, W)),
                 const((2 * LORA_RANK, W)), const((1, W)), const((1, W))]
    args += [wl["rwkv_mu"], wl["rwkv_w0"][d], wl["rwkv_w2"][d], wl["rwkv_a0"][d], wl["rwkv_a2"][d],
             wl["rwkv_k_k"], wl["rwkv_k_a"][d]]
    if final:
        in_specs += [const((1, W)), const((2 * LORA_RANK, W)), const((1, W)), const((GATE_RANK, W)),
                     const((1, W)), const((1, W)), const((1, W))]
        args += [wl["rwkv_a0"][0], wl["rwkv_a2"][0], wl["rwkv_k_a"][0], wl["rwkv_g2"], wl["rwkv_r_k"],
                 wl["rwkv_gn_w"], wl["rwkv_gn_b"]]
    scratch = [pltpu.VMEM((W // HALF, HALF, HALF), F32)]
    scratch += [pltpu.VMEM((tb, W), BF16)] * 7
    scratch += [pltpu.VMEM((tb, W), F32)] * 2
    return pl.pallas_call(
        functools.partial(_rwkv_kernel, reverse=reverse, final=final, tb=tb, nb=nb, chunk=CHUNK),
        grid=(bsz, nb),
        in_specs=in_specs,
        out_specs=pl.BlockSpec((None, tb, W), blk),
        out_shape=jax.ShapeDtypeStruct((bsz, seq, W), BF16 if final else F32),
        scratch_shapes=scratch,
        compiler_params=_params(("arbitrary", "arbitrary")),
        name="rwkv_bwd" if final else "rwkv_fwd",
    )(*args)


def _t5_bucket_table():
    qi = np.arange(ATTN_BLOCK)[:, None]
    kj = np.arange(3 * ATTN_BLOCK)[None, :]
    rel = kj - ATTN_BLOCK - qi
    half = N_BUCKETS // 2
    max_exact = half // 2
    ret = np.where(rel > 0, half, 0)
    n = np.abs(rel)
    nf = np.maximum(n, 1).astype(np.float32)
    large = max_exact + (np.log(nf / max_exact) / math.log(MAX_DISTANCE / max_exact)
                         * (half - max_exact)).astype(np.int32)
    large = np.minimum(large, half - 1)
    return (ret + np.where(n < max_exact, n, large)).astype(np.int32)


def _attn_kernel(q_ref, kp_ref, kc_ref, kn_ref, vp_ref, vc_ref, vn_ref, bkt_ref, rb_ref, sink_ref,
                 o_ref, bias_ref, *, nb):
    b = pl.program_id(0)
    j = pl.program_id(1)
    blk = ATTN_BLOCK

    @pl.when((b == 0) & (j == 0))
    def _():
        bucket = bkt_ref[...]
        for h in range(N_Q_HEADS):
            def body(t, acc, h=h):
                return jnp.where(bucket == t, rb_ref[t, h], acc)
            bias_ref[h] = lax.fori_loop(0, N_BUCKETS, body, jnp.zeros((blk, 3 * blk), F32))

    qi = lax.broadcasted_iota(jnp.int32, (GQA_GROUP * blk, 3 * blk), 0) % blk
    kj = lax.broadcasted_iota(jnp.int32, (GQA_GROUP * blk, 3 * blk), 1)
    rel = kj - blk - qi
    valid4 = (jnp.abs(rel) <= WINDOW) & ((kj >= blk) | (j > 0)) & ((kj < 2 * blk) | (j < nb - 1))
    hrow = lax.broadcasted_iota(jnp.int32, (GQA_GROUP * blk, 1), 0) // blk
    scale = HEAD_DIM ** -0.5
    kcat = jnp.concatenate([kp_ref[...], kc_ref[...], kn_ref[...]], axis=0).astype(BF16)
    vcat = jnp.concatenate([vp_ref[...], vc_ref[...], vn_ref[...]], axis=0).astype(BF16)
    for g in range(N_KV_HEADS):
        qs = jnp.concatenate(
            [q_ref[:, HEAD_DIM * (GQA_GROUP * g + hh):HEAD_DIM * (GQA_GROUP * g + hh + 1)] for hh in range(GQA_GROUP)],
            axis=0).astype(BF16)
        kg = kcat[:, HEAD_DIM * g:HEAD_DIM * (g + 1)]
        vg = vcat[:, HEAD_DIM * g:HEAD_DIM * (g + 1)]
        bias = bias_ref[GQA_GROUP * g:GQA_GROUP * (g + 1)].reshape(GQA_GROUP * blk, 3 * blk)
        s = _dot_nt(qs, kg) * scale
        s = jnp.where(valid4, s + bias, NEG)
        sk = jnp.zeros((GQA_GROUP * blk, 1), F32)
        for hh in range(GQA_GROUP):
            sk = jnp.where(hrow == hh, sink_ref[0, GQA_GROUP * g + hh], sk)
        m = jnp.maximum(jnp.max(s, axis=-1, keepdims=True), sk)
        pexp = jnp.exp(s - m)
        den = jnp.sum(pexp, axis=-1, keepdims=True) + jnp.exp(sk - m)
        o = _dot(pexp.astype(BF16), vg) / den
        for hh in range(GQA_GROUP):
            h = GQA_GROUP * g + hh
            o_ref[:, HEAD_DIM * h:HEAD_DIM * (h + 1)] = o[blk * hh:blk * (hh + 1)].astype(o_ref.dtype)


def _attn(p, wl):
    bsz, seq, _ = p.shape
    blk = ATTN_BLOCK
    nb = seq // blk
    kcol, vcol = P_K // KV_WIDTH, P_V // KV_WIDTH

    def kv(col, off):
        return pl.BlockSpec((None, blk, KV_WIDTH), lambda b, j: (b, jnp.clip(j + off, 0, nb - 1), col))

    smem = lambda: pl.BlockSpec(memory_space=pltpu.SMEM)
    return pl.pallas_call(
        functools.partial(_attn_kernel, nb=nb),
        grid=(bsz, nb),
        in_specs=[pl.BlockSpec((None, blk, ATTN_WIDTH), lambda b, j: (b, j, P_Q // ATTN_WIDTH)),
                  kv(kcol, -1), kv(kcol, 0), kv(kcol, 1), kv(vcol, -1), kv(vcol, 0), kv(vcol, 1),
                  pl.BlockSpec((blk, 3 * blk), lambda b, j: (0, 0)), smem(), smem()],
        out_specs=pl.BlockSpec((None, blk, ATTN_WIDTH), lambda b, j: (b, j, 0)),
        out_shape=jax.ShapeDtypeStruct((bsz, seq, ATTN_WIDTH), BF16),
        scratch_shapes=[pltpu.VMEM((N_Q_HEADS, blk, 3 * blk), F32)],
        compiler_params=_params(("arbitrary", "arbitrary")),
        name="attn",
    )(p, p, p, p, p, p, p, wl["bucket"], wl["rel_bias"], wl["attn_sink"])


def _outproj_kernel(lru_ref, rwkv_ref, attn_ref, x_ref, w_ref, g_ref, o_ref):
    mix = (_dot(lru_ref[...], w_ref[0:LRU_WIDTH, :])
           + _dot(rwkv_ref[...], w_ref[LRU_WIDTH:LRU_WIDTH + RWKV_WIDTH, :])
           + _dot(attn_ref[...], w_ref[LRU_WIDTH + RWKV_WIDTH:, :]))
    ms = jnp.mean(mix * mix, axis=-1, keepdims=True)
    o_ref[...] = x_ref[...] + mix * lax.rsqrt(ms + EPS) * g_ref[...]


def _outproj(lru, rwkv, attn, x2, w, gain):
    t = x2.shape[0]
    tm = min(TM_OUT, t)
    row = lambda width: pl.BlockSpec((tm, width), lambda i: (i, 0))
    return pl.pallas_call(
        _outproj_kernel,
        grid=(t // tm,),
        in_specs=[row(LRU_WIDTH), row(RWKV_WIDTH), row(ATTN_WIDTH), row(D_MODEL),
                  pl.BlockSpec((D_MODEL, D_MODEL), lambda i: (0, 0)),
                  pl.BlockSpec((1, D_MODEL), lambda i: (0, 0))],
        out_specs=row(D_MODEL),
        out_shape=jax.ShapeDtypeStruct((t, D_MODEL), F32),
        compiler_params=_params(("arbitrary",)),
        name="outproj",
    )(lru, rwkv, attn, x2, w, gain)


def _ffn_kernel(x_ref, gpre_ref, wu_ref, wd_ref, gpost_ref, o_ref, h_ref, acc_ref):
    f = pl.program_id(1)

    @pl.when(f == 0)
    def _():
        x = x_ref[...]
        ms = jnp.mean(x * x, axis=-1, keepdims=True)
        h_ref[...] = (x * lax.rsqrt(ms + EPS) * gpre_ref[...]).astype(BF16)
        acc_ref[...] = jnp.zeros_like(acc_ref)

    u = jnp.maximum(_dot(h_ref[...], wu_ref[...]), 0.0)
    acc_ref[...] += _dot((u * u).astype(BF16), wd_ref[...])

    @pl.when(f == pl.num_programs(1) - 1)
    def _():
        y = acc_ref[...]
        ms = jnp.mean(y * y, axis=-1, keepdims=True)
        o_ref[...] = x_ref[...] + y * lax.rsqrt(ms + EPS) * gpost_ref[...]


def _ffn(x2, gpre, wu, wd, gpost):
    t = x2.shape[0]
    tm = min(TM_FFN, t)
    return pl.pallas_call(
        _ffn_kernel,
        grid=(t // tm, D_FF // TF_FFN),
        in_specs=[pl.BlockSpec((tm, D_MODEL), lambda i, f: (i, 0)),
                  pl.BlockSpec((1, D_MODEL), lambda i, f: (0, 0)),
                  pl.BlockSpec((D_MODEL, TF_FFN), lambda i, f: (0, f)),
                  pl.BlockSpec((TF_FFN, D_MODEL), lambda i, f: (f, 0)),
                  pl.BlockSpec((1, D_MODEL), lambda i, f: (0, 0))],
        out_specs=pl.BlockSpec((tm, D_MODEL), lambda i, f: (i, 0)),
        out_shape=jax.ShapeDtypeStruct((t, D_MODEL), F32),
        scratch_shapes=[pltpu.VMEM((tm, D_MODEL), BF16), pltpu.VMEM((tm, D_MODEL), F32)],
        compiler_params=_params(("arbitrary", "arbitrary")),
        name="ffn",
    )(x2, gpre, wu, wd, gpost)


def _block_diag(w):
    n, bi, bj = w.shape
    eye = jnp.eye(n, dtype=w.dtype)
    return (eye[:, None, :, None] * w[:, :, None, :]).reshape(n * bi, n * bj)


def _pad_lora(w2, d):
    z = jnp.zeros_like(w2[d])
    return jnp.concatenate([w2[0], z] if d == 0 else [z, w2[1]], axis=0)


def _layer_weights(l, w):
    row = lambda v: v.reshape(1, -1).astype(F32)
    w_in = w["w_in"][l]
    w_in_p = jnp.concatenate(
        [w_in[:, OFF_RWKV:OFF_ATTN], jnp.zeros((D_MODEL, RWKV_PAD - RWKV_COLS), w_in.dtype),
         w_in[:, 0:OFF_RWKV], w_in[:, OFF_ATTN:]], axis=1).astype(BF16)
    mu = jnp.concatenate([w["rwkv_mu"][l], jnp.zeros((RWKV_PAD - RWKV_COLS,), F32)]).reshape(1, -1)
    return {
        "norm_mix_pre": row(w["norm_mix_pre"][l]), "norm_mix_post": row(w["norm_mix_post"][l]),
        "norm_ffn_pre": row(w["norm_ffn_pre"][l]), "norm_ffn_post": row(w["norm_ffn_post"][l]),
        "w_in": w_in_p, "w_out": w["w_out"][l].astype(BF16),
        "conv_w": w["conv_w"][l].astype(F32), "conv_b": row(w["conv_b"][l]),
        "lru_wa": [_block_diag(w["lru_wa"][l, d]).astype(BF16) for d in range(2)],
        "lru_wx": [_block_diag(w["lru_wx"][l, d]).astype(BF16) for d in range(2)],
        "lru_ba": [row(w["lru_ba"][l, d]) for d in range(2)],
        "lru_bx": [row(w["lru_bx"][l, d]) for d in range(2)],
        "lru_lambda": [row(w["lru_lambda"][l, d]) for d in range(2)],
        "rwkv_mu": mu,
        "rwkv_w0": [row(w["rwkv_w0"][l, d]) for d in range(2)],
        "rwkv_w2": [_pad_lora(w["rwkv_w2"][l], d).astype(BF16) for d in range(2)],
        "rwkv_a0": [row(w["rwkv_a0"][l, d]) for d in range(2)],
        "rwkv_a2": [_pad_lora(w["rwkv_a2"][l], d).astype(BF16) for d in range(2)],
        "rwkv_g2": w["rwkv_g2"][l].astype(BF16),
        "rwkv_k_k": row(w["rwkv_k_k"][l]),
        "rwkv_k_a": [row(w["rwkv_k_a"][l, d]) for d in range(2)],
        "rwkv_r_k": row(w["rwkv_r_k"][l]),
        "rwkv_gn_w": row(w["rwkv_gn_w"][l]), "rwkv_gn_b": row(w["rwkv_gn_b"][l]),
        "attn_sink": w["attn_sink"][l].reshape(1, -1).astype(F32),
        "rel_bias": w["rel_bias"].astype(F32),
        "bucket": jnp.asarray(_t5_bucket_table()),
        "w_up": w["w_up"][l].astype(BF16), "w_down": w["w_down"][l].astype(BF16),
    }


def _trunk(x, layers):
    bsz, seq, _ = x.shape
    t = bsz * seq
    x2 = x.reshape(t, D_MODEL)
    for wl in layers:
        p = _inproj(x2, wl["norm_mix_pre"], wl["w_in"]).reshape(bsz, seq, P_COLS)
        hf = _lru(p, None, wl, 0, False)
        lru_out = _lru(p, hf, wl, 1, True)
        yf = _rwkv(p, None, wl, 0, False)
        rwkv_out = _rwkv(p, yf, wl, 1, True)
        attn_out = _attn(p, wl)
        x2 = _outproj(lru_out.reshape(t, LRU_WIDTH), rwkv_out.reshape(t, RWKV_WIDTH),
                      attn_out.reshape(t, ATTN_WIDTH), x2, wl["w_out"], wl["norm_mix_post"])
        x2 = _ffn(x2, wl["norm_ffn_pre"], wl["w_up"], wl["w_down"], wl["norm_ffn_post"])
    return x2.reshape(bsz, seq, D_MODEL)


def kernel(x_prompt, x_sample, norm_mix_pre, norm_mix_post, norm_ffn_pre, norm_ffn_post, w_in, w_out, conv_w, conv_b, lru_wa, lru_ba, lru_wx, lru_bx, lru_lambda, rwkv_mu, rwkv_w0, rwkv_w2, rwkv_a0, rwkv_a2, rwkv_g2, rwkv_k_k, rwkv_k_a, rwkv_r_k, rwkv_gn_w, rwkv_gn_b, attn_sink, rel_bias, w_up, w_down):
    w = dict(norm_mix_pre=norm_mix_pre, norm_mix_post=norm_mix_post, norm_ffn_pre=norm_ffn_pre,
             norm_ffn_post=norm_ffn_post, w_in=w_in, w_out=w_out, conv_w=conv_w, conv_b=conv_b,
             lru_wa=lru_wa, lru_ba=lru_ba, lru_wx=lru_wx, lru_bx=lru_bx, lru_lambda=lru_lambda,
             rwkv_mu=rwkv_mu, rwkv_w0=rwkv_w0, rwkv_w2=rwkv_w2, rwkv_a0=rwkv_a0, rwkv_a2=rwkv_a2,
             rwkv_g2=rwkv_g2, rwkv_k_k=rwkv_k_k, rwkv_k_a=rwkv_k_a, rwkv_r_k=rwkv_r_k,
             rwkv_gn_w=rwkv_gn_w, rwkv_gn_b=rwkv_gn_b, attn_sink=attn_sink, rel_bias=rel_bias,
             w_up=w_up, w_down=w_down)
    layers = [_layer_weights(l, w) for l in range(w_in.shape[0])]
    return (_trunk(x_prompt, layers), _trunk(x_sample, layers))
```

```python
import functools
import math

import numpy as np
import jax
import jax.numpy as jnp
from jax import lax
from jax.experimental import pallas as pl
from jax.experimental.pallas import tpu as pltpu

F32 = jnp.float32
BF16 = jnp.bfloat16

D_MODEL = 2048
LRU_WIDTH = 512
LRU_BLOCKS = 8
LRU_C = 8.0
RWKV_WIDTH = 512
RWKV_HEAD = 64
RWKV_HEADS = 8
LORA_RANK = 64
GATE_RANK = 128
DECAY_SCALE = 0.606531
GN_EPS = 64e-5
ATTN_WIDTH = 1024
HEAD_DIM = 128
N_Q_HEADS = 8
N_KV_HEADS = 2
GQA_GROUP = 4
KV_WIDTH = 256
WINDOW = 128
ATTN_BLOCK = 128
N_BUCKETS = 32
MAX_DISTANCE = 128
D_FF = 8192
EPS = 1e-6
OFF_RWKV = 2 * LRU_WIDTH
RWKV_COLS = 3 * RWKV_WIDTH + 4 * LORA_RANK + GATE_RANK
OFF_ATTN = OFF_RWKV + RWKV_COLS

RWKV_PAD = 2048
P_LRU_X = 2048
P_LRU_G = 2560
P_Q = 3072
P_K = 4096
P_V = 4352
P_COLS = 4608

LANES = 128
HALF = 256
VMEM_LIMIT = 56 * 1024 * 1024

TM_PROJ = 1024
TN_PROJ = 1536
TM_OUT = 512
TM_FFN = 512
TF_FFN = 1024
TB_LRU = 256
TB_RWKV = 256
CHUNK = 64
NEG = -1e30


def _dot(a, b):
    return jnp.dot(a, b, preferred_element_type=F32)


def _dot_nt(a, b):
    return lax.dot_general(a, b, (((1,), (1,)), ((), ())), preferred_element_type=F32)


def _dot_tn(a, b):
    return lax.dot_general(a, b, (((0,), (0,)), ((), ())), preferred_element_type=F32)


def _split2(x):
    hi = x.astype(BF16)
    lo = (x - hi.astype(F32)).astype(BF16)
    return hi, lo


def _params(sem):
    return pltpu.CompilerParams(dimension_semantics=sem, vmem_limit_bytes=VMEM_LIMIT)


def _inproj_kernel(x_ref, g_ref, w_ref, o_ref, h_ref):
    @pl.when(pl.program_id(1) == 0)
    def _():
        x = x_ref[...]
        ms = jnp.mean(x * x, axis=-1, keepdims=True)
        h_ref[...] = (x * lax.rsqrt(ms + EPS) * g_ref[...]).astype(BF16)

    o_ref[...] = _dot(h_ref[...], w_ref[...])


def _inproj(x2, gain, w):
    t = x2.shape[0]
    tm = min(TM_PROJ, t)
    return pl.pallas_call(
        _inproj_kernel,
        grid=(t // tm, P_COLS // TN_PROJ),
        in_specs=[
            pl.BlockSpec((tm, D_MODEL), lambda i, j: (i, 0)),
            pl.BlockSpec((1, D_MODEL), lambda i, j: (0, 0)),
            pl.BlockSpec((D_MODEL, TN_PROJ), lambda i, j: (0, j)),
        ],
        out_specs=pl.BlockSpec((tm, TN_PROJ), lambda i, j: (i, j)),
        out_shape=jax.ShapeDtypeStruct((t, P_COLS), F32),
        scratch_shapes=[pltpu.VMEM((tm, D_MODEL), BF16)],
        compiler_params=_params(("arbitrary", "arbitrary")),
        name="inproj",
    )(x2, gain, w)


def _time_block(j, nb, reverse):
    return nb - 1 - j if reverse else j


def _halo_specs(tb, nb, width, col, reverse):
    rows8 = tb // 8
    last8 = nb * rows8 - 1

    def blk(b, j):
        return (b, _time_block(j, nb, reverse), col)

    def prev(b, j):
        return (b, jnp.maximum(_time_block(j, nb, reverse) * rows8 - 1, 0), col)

    def nxt(b, j):
        return (b, jnp.minimum((_time_block(j, nb, reverse) + 1) * rows8, last8), col)

    return [pl.BlockSpec((None, tb, width), blk),
            pl.BlockSpec((None, 8, width), prev),
            pl.BlockSpec((None, 8, width), nxt)]


def _shift_rows(x, prev8, next8, row, tb):
    xm1 = jnp.where(row == 0, prev8[7:8], pltpu.roll(x, 1, 0))
    xm2 = jnp.where(row == 0, prev8[6:7], jnp.where(row == 1, prev8[7:8], pltpu.roll(x, 2, 0)))
    xp1 = jnp.where(row == tb - 1, next8[0:1], pltpu.roll(x, tb - 1, 0))
    return xm2, xm1, xp1


def _softplus(z):
    return jnp.maximum(z, 0.0) + jnp.log(1.0 + jnp.exp(-jnp.abs(z)))


def _lru_kernel(*refs, reverse, final, tb, nb, seq):
    if final:
        (x_ref, xp_ref, xn_ref, g_ref, hf_ref, cw_ref, cb_ref, wa_ref, wx_ref, ba_ref, bx_ref, lam_ref,
         o_ref, carry_ref) = refs
    else:
        (x_ref, xp_ref, xn_ref, cw_ref, cb_ref, wa_ref, wx_ref, ba_ref, bx_ref, lam_ref,
         o_ref, carry_ref) = refs
    j = pl.program_id(1)
    jj = _time_block(j, nb, reverse)

    @pl.when(j == 0)
    def _():
        carry_ref[...] = jnp.zeros_like(carry_ref)

    x = x_ref[...]
    prev8 = jnp.where(jj > 0, xp_ref[...], 0.0)
    next8 = jnp.where(jj < nb - 1, xn_ref[...], 0.0)
    row = lax.broadcasted_iota(jnp.int32, (tb, 1), 0)
    xm2, xm1, xp1 = _shift_rows(x, prev8, next8, row, tb)
    cw = cw_ref[...]
    xc = cw[0:1] * xm2 + cw[1:2] * xm1 + cw[2:3] * x + cw[3:4] * xp1 + cb_ref[...]

    xcb = xc.astype(BF16)
    r = jax.nn.sigmoid(_dot(xcb, wa_ref[...]) + ba_ref[...])
    i = jax.nn.sigmoid(_dot(xcb, wx_ref[...]) + bx_ref[...])
    log_a = -LRU_C * r * _softplus(-lam_ref[...])
    a = jnp.exp(log_a)
    om = 1.0 - a * a
    mult = jnp.where(om > 0.0, om * lax.rsqrt(om), 0.0)
    gpos = jj * tb + row
    first = (gpos == seq - 1) if reverse else (gpos == 0)
    mult = jnp.where(first, 1.0, mult)
    u = mult * i * xc

    acc_a, acc_u = a, u
    s = 1
    while s < tb:
        if reverse:
            sh_a, sh_u, valid = pltpu.roll(acc_a, tb - s, 0), pltpu.roll(acc_u, tb - s, 0), row < tb - s
        else:
            sh_a, sh_u, valid = pltpu.roll(acc_a, s, 0), pltpu.roll(acc_u, s, 0), row >= s
        acc_u = acc_a * jnp.where(valid, sh_u, 0.0) + acc_u
        acc_a = acc_a * jnp.where(valid, sh_a, 1.0)
        s *= 2
    h = acc_a * carry_ref[0:1, :] + acc_u
    last = h[0:1] if reverse else h[tb - 1:tb]
    carry_ref[...] = jnp.broadcast_to(last, carry_ref.shape)

    if final:
        o_ref[...] = ((hf_ref[...] + h) * jax.nn.gelu(g_ref[...], approximate=True)).astype(o_ref.dtype)
    else:
        o_ref[...] = h


def _lru(p, hf, wl, d, final):
    bsz, seq, _ = p.shape
    tb = min(TB_LRU, seq)
    nb = seq // tb
    reverse = d == 1
    in_specs = _halo_specs(tb, nb, LRU_WIDTH, P_LRU_X // LRU_WIDTH, reverse)
    args = [p, p, p]
    blk = lambda b, j: (b, _time_block(j, nb, reverse), 0)
    if final:
        in_specs += [pl.BlockSpec((None, tb, LRU_WIDTH), lambda b, j: (b, _time_block(j, nb, reverse), P_LRU_G // LRU_WIDTH)),
                     pl.BlockSpec((None, tb, LRU_WIDTH), blk)]
        args += [p, hf]
    const = lambda shape: pl.BlockSpec(shape, lambda b, j: (0,) * len(shape))
    in_specs += [const((4, LRU_WIDTH)), const((1, LRU_WIDTH)), const((LRU_WIDTH, LRU_WIDTH)),
                 const((LRU_WIDTH, LRU_WIDTH)), const((1, LRU_WIDTH)), const((1, LRU_WIDTH)), const((1, LRU_WIDTH))]
    args += [wl["conv_w"], wl["conv_b"], wl["lru_wa"][d], wl["lru_wx"][d], wl["lru_ba"][d], wl["lru_bx"][d],
             wl["lru_lambda"][d]]
    return pl.pallas_call(
        functools.partial(_lru_kernel, reverse=reverse, final=final, tb=tb, nb=nb, seq=seq),
        grid=(bsz, nb),
        in_specs=in_specs,
        out_specs=pl.BlockSpec((None, tb, LRU_WIDTH), blk),
        out_shape=jax.ShapeDtypeStruct((bsz, seq, LRU_WIDTH), BF16 if final else F32),
        scratch_shapes=[pltpu.VMEM((8, LRU_WIDTH), F32)],
        compiler_params=_params(("arbitrary", "arbitrary")),
        name="lru_bwd" if final else "lru_fwd",
    )(*args)


def _seg_sum(x, ones_bd):
    xb = x.astype(BF16)
    return jnp.concatenate(
        [_dot(xb[:, HALF * hf:HALF * (hf + 1)], ones_bd) for hf in range(x.shape[1] // HALF)], axis=1)


def _rwkv_kernel(*refs, reverse, final, tb, nb, chunk):
    if final:
        (z_ref, zp_ref, zn_ref, yf_ref, mu_ref, w0_ref, w2_ref, a0_ref, a2_ref, kk_ref, ka_ref,
         a0f_ref, a2f_ref, kaf_ref, g2_ref, rk_ref, gnw_ref, gnb_ref,
         o_ref, s_ref, qk_s, rt_s, kt_s, bt_s, ktg_s, btg_s, v_s, rp_s, egl_s, y_s, en_s, f_s) = refs
    else:
        (z_ref, zp_ref, zn_ref, mu_ref, w0_ref, w2_ref, a0_ref, a2_ref, kk_ref, ka_ref,
         o_ref, s_ref, qk_s, rt_s, kt_s, bt_s, ktg_s, btg_s, v_s, rp_s, egl_s, y_s, en_s, f_s) = refs
    j = pl.program_id(1)
    jj = _time_block(j, nb, reverse)
    W = RWKV_WIDTH
    nchunk = tb // chunk
    group = HALF // chunk

    @pl.when(j == 0)
    def _():
        s_ref[...] = jnp.zeros_like(s_ref)

    z = z_ref[...]
    prev8 = jnp.where(jj > 0, zp_ref[...], 0.0)
    next8 = jnp.where(jj < nb - 1, zn_ref[...], 0.0)
    row = lax.broadcasted_iota(jnp.int32, (tb, 1), 0)
    zm1 = jnp.where(row == 0, prev8[7:8], pltpu.roll(z, 1, 0))
    zp1 = jnp.where(row == tb - 1, next8[0:1], pltpu.roll(z, tb - 1, 0))
    zs = z + (0.5 * (zm1 + zp1) - z) * mu_ref[...]
    r = zs[:, 0:W]
    k = zs[:, W:2 * W]
    v = zs[:, 2 * W:3 * W]
    dw = zs[:, 3 * W:3 * W + 2 * LORA_RANK]
    da = zs[:, 3 * W + 2 * LORA_RANK:3 * W + 4 * LORA_RANK]
    dg = zs[:, 3 * W + 4 * LORA_RANK:3 * W + 4 * LORA_RANK + GATE_RANK]

    lw = -DECAY_SCALE * jax.nn.sigmoid(_dot(jnp.tanh(dw).astype(BF16), w2_ref[...]) + w0_ref[...])
    dab = da.astype(BF16)
    a = jax.nn.sigmoid(_dot(dab, a2_ref[...]) + a0_ref[...])
    kdir = k * (1.0 + (a - 1.0) * ka_ref[...])
    ci = lax.broadcasted_iota(jnp.int32, (HALF, HALF), 0) // RWKV_HEAD
    cj = lax.broadcasted_iota(jnp.int32, (HALF, HALF), 1) // RWKV_HEAD
    ones_bd = jnp.where(ci == cj, 1.0, 0.0).astype(BF16)
    kk = k * kk_ref[...]
    kk = kk * lax.rsqrt(jnp.maximum(_seg_sum(kk * kk, ones_bd), 1e-24))
    b = kk * a

    ti = lax.broadcasted_iota(jnp.int32, (tb, tb), 0)
    tj = lax.broadcasted_iota(jnp.int32, (tb, tb), 1)
    same = (ti // chunk) == (tj // chunk)
    before = (tj >= ti) if reverse else (tj <= ti)
    tri = jnp.where(same & before, 1.0, 0.0).astype(BF16)
    tot = jnp.where(same, 1.0, 0.0).astype(BF16)
    hi, lo = _split2(lw)
    g = _dot(tri, hi) + _dot(tri, lo)
    gl = _dot(tot, hi) + _dot(tot, lo)
    e_g = jnp.exp(g)
    e_ng = jnp.exp(-g)
    e_gl = jnp.exp(gl)
    kt = kdir * e_ng
    bt = b * e_ng
    qk_s[...] = (kk * jnp.exp(g - lw)).astype(BF16)
    rt_s[...] = (r * e_g).astype(BF16)
    kt_s[...] = kt.astype(BF16)
    bt_s[...] = bt.astype(BF16)
    ktg_s[...] = (kt * e_gl).astype(BF16)
    btg_s[...] = (-(bt * e_gl)).astype(BF16)
    v_s[...] = v.astype(BF16)
    egl_s[...] = e_gl

    hr = lax.broadcasted_iota(jnp.int32, (HALF, HALF), 0)
    hc = lax.broadcasted_iota(jnp.int32, (HALF, HALF), 1)
    head_mask = jnp.where((hr // chunk) == (hc // RWKV_HEAD), 1.0, 0.0).astype(BF16)
    bd_mask = jnp.where((hr // chunk) == (hc // chunk), 1.0, 0.0).astype(BF16)
    st_mask = (hr // RWKV_HEAD) == (hc // RWKV_HEAD)
    wi = lax.broadcasted_iota(jnp.int32, (chunk, HALF), 0)
    wj = lax.broadcasted_iota(jnp.int32, (chunk, HALF), 1) % chunk
    strict = (wj > wi) if reverse else (wj < wi)
    incl = (wj >= wi) if reverse else (wj <= wi)
    eye_w = jnp.where(wi == wj, 1.0, 0.0)

    def stack(xb):
        return jnp.concatenate([xb] * group, axis=0) * head_mask

    def blockdiag(xw):
        return jnp.concatenate([xw.astype(BF16)] * group, axis=0) * bd_mask

    n_levels = int(math.log2(chunk))
    halves = W // HALF
    order = range(nchunk - 1, -1, -1) if reverse else range(nchunk)

    chains = [(c, hf) for c in order for hf in range(halves)]

    def rows(c):
        return slice(c * chunk, (c + 1) * chunk)

    def lanes(hf):
        return slice(HALF * hf, HALF * hf + HALF)

    gram = []
    for c, hf in chains:
        sl, ls = rows(c), lanes(hf)
        x2 = jnp.concatenate([qk_s[sl, ls], rt_s[sl, ls]], axis=0)
        gk = _dot_nt(x2, stack(kt_s[sl, ls]))
        gb = _dot_nt(x2, stack(bt_s[sl, ls]))
        gram.append((gk, gb))
    a_k, b_k, b_b, n_w, pw, inv = [], [], [], [], [], []
    for gk, gb in gram:
        a_k.append(jnp.where(strict, gk[:chunk], 0.0).astype(BF16))
        b_k.append(jnp.where(incl, gk[chunk:], 0.0).astype(BF16))
        b_b.append(jnp.where(incl, gb[chunk:], 0.0).astype(BF16))
        n_w.append(jnp.where(strict, -gb[:chunk], 0.0))
    for n in n_w:
        pw.append(_dot(n.astype(BF16), blockdiag(n)))
        inv.append(eye_w + n)
    for level in range(1, n_levels):
        for i in range(len(chains)):
            bdp = blockdiag(pw[i])
            if level < n_levels - 1:
                both = _dot(jnp.concatenate([pw[i], inv[i]], axis=0).astype(BF16), bdp)
                pw[i] = both[:chunk]
                inv[i] = inv[i] + both[chunk:]
            else:
                inv[i] = inv[i] + _dot(inv[i].astype(BF16), bdp)
    tw = [t.astype(BF16) for t in inv]
    vm = [stack(v_s[rows(c), lanes(hf)]) for c, hf in chains]
    akv = [_dot(a_k[i], vm[i]) for i in range(len(chains))]
    qp = [_dot(tw[i], stack(qk_s[rows(c), lanes(hf)])).astype(BF16) for i, (c, hf) in enumerate(chains)]
    vp = [_dot(tw[i], stack(akv[i].astype(BF16))).astype(BF16) for i in range(len(chains))]
    for i, (c, hf) in enumerate(chains):
        sl, ls = rows(c), lanes(hf)
        rp_s[sl, ls] = (rt_s[sl, ls].astype(F32) - _dot(b_b[i], stack(qp[i]))).astype(BF16)
    for i, (c, hf) in enumerate(chains):
        y_s[rows(c), lanes(hf)] = _dot(b_k[i], vm[i]) - _dot(b_b[i], stack(vp[i]))
    for i, (c, hf) in enumerate(chains):
        en = _dot_tn(qp[i], btg_s[rows(c), lanes(hf)])
        en_s[c * halves + hf] = jnp.where(st_mask, en, 0.0).astype(BF16)
    for i, (c, hf) in enumerate(chains):
        sl, ls = rows(c), lanes(hf)
        fm = _dot_tn(jnp.concatenate([v_s[sl, ls], vp[i]], axis=0),
                     jnp.concatenate([ktg_s[sl, ls], btg_s[sl, ls]], axis=0))
        f_s[c * halves + hf] = jnp.where(st_mask, fm, 0.0)

    for c in order:
        sl = slice(c * chunk, (c + 1) * chunk)
        for hf in range(halves):
            ls = slice(HALF * hf, HALF * hf + HALF)
            st = s_ref[hf]
            stb = st.astype(BF16)
            y_s[sl, ls] = y_s[sl, ls] + _dot_nt(rp_s[sl, ls], stb)
            decay = egl_s[sl, ls][0:1]
            s_ref[hf] = st * decay + _dot(stb, en_s[c * halves + hf]) + f_s[c * halves + hf]

    if not final:
        o_ref[...] = y_s[...]
    else:
        y = yf_ref[...] + y_s[...]
        inv_n = 1.0 / RWKV_HEAD
        mean = _seg_sum(y, ones_bd) * inv_n
        yc = y - mean
        var = _seg_sum(yc * yc, ones_bd) * inv_n
        yn = yc * lax.rsqrt(var + GN_EPS) * gnw_ref[...] + gnb_ref[...]
        a_f = jax.nn.sigmoid(_dot(dab, a2f_ref[...]) + a0f_ref[...])
        kdir_f = k * (1.0 + (a_f - 1.0) * kaf_ref[...])
        bonus = _seg_sum(r * (kdir_f + kdir) * rk_ref[...], ones_bd) * v
        gate = _dot(jax.nn.sigmoid(dg).astype(BF16), g2_ref[...])
        o_ref[...] = ((yn + bonus) * gate).astype(o_ref.dtype)


def _rwkv(p, yf, wl, d, final):
    bsz, seq, _ = p.shape
    tb = min(TB_RWKV, seq)
    nb = seq // tb
    reverse = d == 1
    W = RWKV_WIDTH
    in_specs = _halo_specs(tb, nb, RWKV_PAD, 0, reverse)
    args = [p, p, p]
    blk = lambda b, j: (b, _time_block(j, nb, reverse), 0)
    const = lambda shape: pl.BlockSpec(shape, lambda b, j: (0,) * len(shape))
    if final:
        in_specs += [pl.BlockSpec((None, tb, W), blk)]
        args += [yf]
    in_specs += [const((1, RWKV_PAD)), const((1, W)), const((2 * LORA_RANK, W)), const((1, W)),
                 const((2 * LORA_RANK, W)), const((1, W)), const((1, W))]
    args += [wl["rwkv_mu"], wl["rwkv_w0"][d], wl["rwkv_w2"][d], wl["rwkv_a0"][d], wl["rwkv_a2"][d],
             wl["rwkv_k_k"], wl["rwkv_k_a"][d]]
    if final:
        in_specs += [const((1, W)), const((2 * LORA_RANK, W)), const((1, W)), const((GATE_RANK, W)),
                     const((1, W)), const((1, W)), const((1, W))]
        args += [wl["rwkv_a0"][0], wl["rwkv_a2"][0], wl["rwkv_k_a"][0], wl["rwkv_g2"], wl["rwkv_r_k"],
                 wl["rwkv_gn_w"], wl["rwkv_gn_b"]]
    scratch = [pltpu.VMEM((W // HALF, HALF, HALF), F32)]
    scratch += [pltpu.VMEM((tb, W), BF16)] * 8
    scratch += [pltpu.VMEM((tb, W), F32)] * 2
    nhc = (tb // CHUNK) * (W // HALF)
    scratch += [pltpu.VMEM((nhc, HALF, HALF), BF16), pltpu.VMEM((nhc, HALF, HALF), F32)]
    return pl.pallas_call(
        functools.partial(_rwkv_kernel, reverse=reverse, final=final, tb=tb, nb=nb, chunk=CHUNK),
        grid=(bsz, nb),
        in_specs=in_specs,
        out_specs=pl.BlockSpec((None, tb, W), blk),
        out_shape=jax.ShapeDtypeStruct((bsz, seq, W), BF16 if final else F32),
        scratch_shapes=scratch,
        compiler_params=_params(("arbitrary", "arbitrary")),
        name="rwkv_bwd" if final else "rwkv_fwd",
    )(*args)


def _t5_bucket_table():
    qi = np.arange(ATTN_BLOCK)[:, None]
    kj = np.arange(3 * ATTN_BLOCK)[None, :]
    rel = kj - ATTN_BLOCK - qi
    half = N_BUCKETS // 2
    max_exact = half // 2
    ret = np.where(rel > 0, half, 0)
    n = np.abs(rel)
    nf = np.maximum(n, 1).astype(np.float32)
    large = max_exact + (np.log(nf / max_exact) / math.log(MAX_DISTANCE / max_exact)
                         * (half - max_exact)).astype(np.int32)
    large = np.minimum(large, half - 1)
    return (ret + np.where(n < max_exact, n, large)).astype(np.int32)


def _attn_kernel(q_ref, kp_ref, kc_ref, kn_ref, vp_ref, vc_ref, vn_ref, bkt_ref, rb_ref, sink_ref,
                 o_ref, bias_ref, *, nb):
    b = pl.program_id(0)
    j = pl.program_id(1)
    blk = ATTN_BLOCK

    @pl.when((b == 0) & (j == 0))
    def _():
        bucket = bkt_ref[...]
        for h in range(N_Q_HEADS):
            def body(t, acc, h=h):
                return jnp.where(bucket == t, rb_ref[t, h], acc)
            bias_ref[h] = lax.fori_loop(0, N_BUCKETS, body, jnp.zeros((blk, 3 * blk), F32))

    qi = lax.broadcasted_iota(jnp.int32, (GQA_GROUP * blk, 3 * blk), 0) % blk
    kj = lax.broadcasted_iota(jnp.int32, (GQA_GROUP * blk, 3 * blk), 1)
    rel = kj - blk - qi
    valid4 = (jnp.abs(rel) <= WINDOW) & ((kj >= blk) | (j > 0)) & ((kj < 2 * blk) | (j < nb - 1))
    hrow = lax.broadcasted_iota(jnp.int32, (GQA_GROUP * blk, 1), 0) // blk
    scale = HEAD_DIM ** -0.5
    kcat = jnp.concatenate([kp_ref[...], kc_ref[...], kn_ref[...]], axis=0).astype(BF16)
    vcat = jnp.concatenate([vp_ref[...], vc_ref[...], vn_ref[...]], axis=0).astype(BF16)
    for g in range(N_KV_HEADS):
        qs = jnp.concatenate(
            [q_ref[:, HEAD_DIM * (GQA_GROUP * g + hh):HEAD_DIM * (GQA_GROUP * g + hh + 1)] for hh in range(GQA_GROUP)],
            axis=0).astype(BF16)
        kg = kcat[:, HEAD_DIM * g:HEAD_DIM * (g + 1)]
        vg = vcat[:, HEAD_DIM * g:HEAD_DIM * (g + 1)]
        bias = bias_ref[GQA_GROUP * g:GQA_GROUP * (g + 1)].reshape(GQA_GROUP * blk, 3 * blk)
        s = _dot_nt(qs, kg) * scale
        s = jnp.where(valid4, s + bias, NEG)
        sk = jnp.zeros((GQA_GROUP * blk, 1), F32)
        for hh in range(GQA_GROUP):
            sk = jnp.where(hrow == hh, sink_ref[0, GQA_GROUP * g + hh], sk)
        m = jnp.maximum(jnp.max(s, axis=-1, keepdims=True), sk)
        pexp = jnp.exp(s - m)
        den = jnp.sum(pexp, axis=-1, keepdims=True) + jnp.exp(sk - m)
        o = _dot(pexp.astype(BF16), vg) / den
        for hh in range(GQA_GROUP):
            h = GQA_GROUP * g + hh
            o_ref[:, HEAD_DIM * h:HEAD_DIM * (h + 1)] = o[blk * hh:blk * (hh + 1)].astype(o_ref.dtype)


def _attn(p, wl):
    bsz, seq, _ = p.shape
    blk = ATTN_BLOCK
    nb = seq // blk
    kcol, vcol = P_K // KV_WIDTH, P_V // KV_WIDTH

    def kv(col, off):
        return pl.BlockSpec((None, blk, KV_WIDTH), lambda b, j: (b, jnp.clip(j + off, 0, nb - 1), col))

    smem = lambda: pl.BlockSpec(memory_space=pltpu.SMEM)
    return pl.pallas_call(
        functools.partial(_attn_kernel, nb=nb),
        grid=(bsz, nb),
        in_specs=[pl.BlockSpec((None, blk, ATTN_WIDTH), lambda b, j: (b, j, P_Q // ATTN_WIDTH)),
                  kv(kcol, -1), kv(kcol, 0), kv(kcol, 1), kv(vcol, -1), kv(vcol, 0), kv(vcol, 1),
                  pl.BlockSpec((blk, 3 * blk), lambda b, j: (0, 0)), smem(), smem()],
        out_specs=pl.BlockSpec((None, blk, ATTN_WIDTH), lambda b, j: (b, j, 0)),
        out_shape=jax.ShapeDtypeStruct((bsz, seq, ATTN_WIDTH), BF16),
        scratch_shapes=[pltpu.VMEM((N_Q_HEADS, blk, 3 * blk), F32)],
        compiler_params=_params(("arbitrary", "arbitrary")),
        name="attn",
    )(p, p, p, p, p, p, p, wl["bucket"], wl["rel_bias"], wl["attn_sink"])


def _outproj_kernel(lru_ref, rwkv_ref, attn_ref, x_ref, w_ref, g_ref, o_ref):
    mix = (_dot(lru_ref[...], w_ref[0:LRU_WIDTH, :])
           + _dot(rwkv_ref[...], w_ref[LRU_WIDTH:LRU_WIDTH + RWKV_WIDTH, :])
           + _dot(attn_ref[...], w_ref[LRU_WIDTH + RWKV_WIDTH:, :]))
    ms = jnp.mean(mix * mix, axis=-1, keepdims=True)
    o_ref[...] = x_ref[...] + mix * lax.rsqrt(ms + EPS) * g_ref[...]


def _outproj(lru, rwkv, attn, x2, w, gain):
    t = x2.shape[0]
    tm = min(TM_OUT, t)
    row = lambda width: pl.BlockSpec((tm, width), lambda i: (i, 0))
    return pl.pallas_call(
        _outproj_kernel,
        grid=(t // tm,),
        in_specs=[row(LRU_WIDTH), row(RWKV_WIDTH), row(ATTN_WIDTH), row(D_MODEL),
                  pl.BlockSpec((D_MODEL, D_MODEL), lambda i: (0, 0)),
                  pl.BlockSpec((1, D_MODEL), lambda i: (0, 0))],
        out_specs=row(D_MODEL),
        out_shape=jax.ShapeDtypeStruct((t, D_MODEL), F32),
        compiler_params=_params(("arbitrary",)),
        name="outproj",
    )(lru, rwkv, attn, x2, w, gain)


def _ffn_kernel(x_ref, gpre_ref, wu_ref, wd_ref, gpost_ref, o_ref, h_ref, acc_ref):
    f = pl.program_id(1)

    @pl.when(f == 0)
    def _():
        x = x_ref[...]
        ms = jnp.mean(x * x, axis=-1, keepdims=True)
        h_ref[...] = (x * lax.rsqrt(ms + EPS) * gpre_ref[...]).astype(BF16)
        acc_ref[...] = jnp.zeros_like(acc_ref)

    u = jnp.maximum(_dot(h_ref[...], wu_ref[...]), 0.0)
    acc_ref[...] += _dot((u * u).astype(BF16), wd_ref[...])

    @pl.when(f == pl.num_programs(1) - 1)
    def _():
        y = acc_ref[...]
        ms = jnp.mean(y * y, axis=-1, keepdims=True)
        o_ref[...] = x_ref[...] + y * lax.rsqrt(ms + EPS) * gpost_ref[...]


def _ffn(x2, gpre, wu, wd, gpost):
    t = x2.shape[0]
    tm = min(TM_FFN, t)
    return pl.pallas_call(
        _ffn_kernel,
        grid=(t // tm, D_FF // TF_FFN),
        in_specs=[pl.BlockSpec((tm, D_MODEL), lambda i, f: (i, 0)),
                  pl.BlockSpec((1, D_MODEL), lambda i, f: (0, 0)),
                  pl.BlockSpec((D_MODEL, TF_FFN), lambda i, f: (0, f)),
                  pl.BlockSpec((TF_FFN, D_MODEL), lambda i, f: (f, 0)),
                  pl.BlockSpec((1, D_MODEL), lambda i, f: (0, 0))],
        out_specs=pl.BlockSpec((tm, D_MODEL), lambda i, f: (i, 0)),
        out_shape=jax.ShapeDtypeStruct((t, D_MODEL), F32),
        scratch_shapes=[pltpu.VMEM((tm, D_MODEL), BF16), pltpu.VMEM((tm, D_MODEL), F32)],
        compiler_params=_params(("arbitrary", "arbitrary")),
        name="ffn",
    )(x2, gpre, wu, wd, gpost)


def _block_diag(w):
    n, bi, bj = w.shape
    eye = jnp.eye(n, dtype=w.dtype)
    return (eye[:, None, :, None] * w[:, :, None, :]).reshape(n * bi, n * bj)


def _pad_lora(w2, d):
    z = jnp.zeros_like(w2[d])
    return jnp.concatenate([w2[0], z] if d == 0 else [z, w2[1]], axis=0)


def _layer_weights(l, w):
    row = lambda v: v.reshape(1, -1).astype(F32)
    w_in = w["w_in"][l]
    w_in_p = jnp.concatenate(
        [w_in[:, OFF_RWKV:OFF_ATTN], jnp.zeros((D_MODEL, RWKV_PAD - RWKV_COLS), w_in.dtype),
         w_in[:, 0:OFF_RWKV], w_in[:, OFF_ATTN:]], axis=1).astype(BF16)
    mu = jnp.concatenate([w["rwkv_mu"][l], jnp.zeros((RWKV_PAD - RWKV_COLS,), F32)]).reshape(1, -1)
    return {
        "norm_mix_pre": row(w["norm_mix_pre"][l]), "norm_mix_post": row(w["norm_mix_post"][l]),
        "norm_ffn_pre": row(w["norm_ffn_pre"][l]), "norm_ffn_post": row(w["norm_ffn_post"][l]),
        "w_in": w_in_p, "w_out": w["w_out"][l].astype(BF16),
        "conv_w": w["conv_w"][l].astype(F32), "conv_b": row(w["conv_b"][l]),
        "lru_wa": [_block_diag(w["lru_wa"][l, d]).astype(BF16) for d in range(2)],
        "lru_wx": [_block_diag(w["lru_wx"][l, d]).astype(BF16) for d in range(2)],
        "lru_ba": [row(w["lru_ba"][l, d]) for d in range(2)],
        "lru_bx": [row(w["lru_bx"][l, d]) for d in range(2)],
        "lru_lambda": [row(w["lru_lambda"][l, d]) for d in range(2)],
        "rwkv_mu": mu,
        "rwkv_w0": [row(w["rwkv_w0"][l, d]) for d in range(2)],
        "rwkv_w2": [_pad_lora(w["rwkv_w2"][l], d).astype(BF16) for d in range(2)],
        "rwkv_a0": [row(w["rwkv_a0"][l, d]) for d in range(2)],
        "rwkv_a2": [_pad_lora(w["rwkv_a2"][l], d).astype(BF16) for d in range(2)],
        "rwkv_g2": w["rwkv_g2"][l].astype(BF16),
        "rwkv_k_k": row(w["rwkv_k_k"][l]),
        "rwkv_k_a": [row(w["rwkv_k_a"][l, d]) for d in range(2)],
        "rwkv_r_k": row(w["rwkv_r_k"][l]),
        "rwkv_gn_w": row(w["rwkv_gn_w"][l]), "rwkv_gn_b": row(w["rwkv_gn_b"][l]),
        "attn_sink": w["attn_sink"][l].reshape(1, -1).astype(F32),
        "rel_bias": w["rel_bias"].astype(F32),
        "bucket": jnp.asarray(_t5_bucket_table()),
        "w_up": w["w_up"][l].astype(BF16), "w_down": w["w_down"][l].astype(BF16),
    }


def _trunk(x, layers):
    bsz, seq, _ = x.shape
    t = bsz * seq
    x2 = x.reshape(t, D_MODEL)
    for wl in layers:
        p = _inproj(x2, wl["norm_mix_pre"], wl["w_in"]).reshape(bsz, seq, P_COLS)
        hf = _lru(p, None, wl, 0, False)
        lru_out = _lru(p, hf, wl, 1, True)
        yf = _rwkv(p, None, wl, 0, False)
        rwkv_out = _rwkv(p, yf, wl, 1, True)
        attn_out = _attn(p, wl)
        x2 = _outproj(lru_out.reshape(t, LRU_WIDTH), rwkv_out.reshape(t, RWKV_WIDTH),
                      attn_out.reshape(t, ATTN_WIDTH), x2, wl["w_out"], wl["norm_mix_post"])
        x2 = _ffn(x2, wl["norm_ffn_pre"], wl["w_up"], wl["w_down"], wl["norm_ffn_post"])
    return x2.reshape(bsz, seq, D_MODEL)


def kernel(x_prompt, x_sample, norm_mix_pre, norm_mix_post, norm_ffn_pre, norm_ffn_post, w_in, w_out, conv_w, conv_b, lru_wa, lru_ba, lru_wx, lru_bx, lru_lambda, rwkv_mu, rwkv_w0, rwkv_w2, rwkv_a0, rwkv_a2, rwkv_g2, rwkv_k_k, rwkv_k_a, rwkv_r_k, rwkv_gn_w, rwkv_gn_b, attn_sink, rel_bias, w_up, w_down):
    w = dict(norm_mix_pre=norm_mix_pre, norm_mix_post=norm_mix_post, norm_ffn_pre=norm_ffn_pre,
             norm_ffn_post=norm_ffn_post, w_in=w_in, w_out=w_out, conv_w=conv_w, conv_b=conv_b,
             lru_wa=lru_wa, lru_ba=lru_ba, lru_wx=lru_wx, lru_bx=lru_bx, lru_lambda=lru_lambda,
             rwkv_mu=rwkv_mu, rwkv_w0=rwkv_w0, rwkv_w2=rwkv_w2, rwkv_a0=rwkv_a0, rwkv_a2=rwkv_a2,
             rwkv_g2=rwkv_g2, rwkv_k_k=rwkv_k_k, rwkv_k_a=rwkv_k_a, rwkv_r_k=rwkv_r_k,
             rwkv_gn_w=rwkv_gn_w, rwkv_gn_b=rwkv_gn_b, attn_sink=attn_sink, rel_bias=rel_bias,
             w_up=w_up, w_down=w_down)
    layers = [_layer_weights(l, w) for l in range(w_in.shape[0])]
    return (_trunk(x_prompt, layers), _trunk(x_sample, layers))
```

```python
import functools
import math

import numpy as np
import jax
import jax.numpy as jnp
from jax import lax
from jax.experimental import pallas as pl
from jax.experimental.pallas import tpu as pltpu

F32 = jnp.float32
BF16 = jnp.bfloat16

D_MODEL = 2048
LRU_WIDTH = 512
LRU_BLOCKS = 8
LRU_C = 8.0
RWKV_WIDTH = 512
RWKV_HEAD = 64
RWKV_HEADS = 8
LORA_RANK = 64
GATE_RANK = 128
DECAY_SCALE = 0.606531
GN_EPS = 64e-5
ATTN_WIDTH = 1024
HEAD_DIM = 128
N_Q_HEADS = 8
N_KV_HEADS = 2
GQA_GROUP = 4
KV_WIDTH = 256
WINDOW = 128
ATTN_BLOCK = 128
N_BUCKETS = 32
MAX_DISTANCE = 128
D_FF = 8192
EPS = 1e-6
OFF_RWKV = 2 * LRU_WIDTH
RWKV_COLS = 3 * RWKV_WIDTH + 4 * LORA_RANK + GATE_RANK
OFF_ATTN = OFF_RWKV + RWKV_COLS

RWKV_PAD = 2048
P_LRU_X = 2048
P_LRU_G = 2560
P_Q = 3072
P_K = 4096
P_V = 4352
P_COLS = 4608

LANES = 128
HALF = 256
VMEM_LIMIT = 56 * 1024 * 1024

TM_PROJ = 1024
TN_PROJ = 1536
TM_OUT = 512
TM_FFN = 512
TF_FFN = 1024
TB_LRU = 256
TB_RWKV = 512
SUB_RWKV = 256
CHUNK = 64
NEG = -1e30


def _dot(a, b):
    return jnp.dot(a, b, preferred_element_type=F32)


def _dot_nt(a, b):
    return lax.dot_general(a, b, (((1,), (1,)), ((), ())), preferred_element_type=F32)


def _dot_tn(a, b):
    return lax.dot_general(a, b, (((0,), (0,)), ((), ())), preferred_element_type=F32)


def _split2(x):
    hi = x.astype(BF16)
    lo = (x - hi.astype(F32)).astype(BF16)
    return hi, lo


def _params(sem):
    return pltpu.CompilerParams(dimension_semantics=sem, vmem_limit_bytes=VMEM_LIMIT)


def _inproj_kernel(x_ref, g_ref, w_ref, o_ref, h_ref):
    @pl.when(pl.program_id(1) == 0)
    def _():
        x = x_ref[...]
        ms = jnp.mean(x * x, axis=-1, keepdims=True)
        h_ref[...] = (x * lax.rsqrt(ms + EPS) * g_ref[...]).astype(BF16)

    o_ref[...] = _dot(h_ref[...], w_ref[...])


def _inproj(x2, gain, w):
    t = x2.shape[0]
    tm = min(TM_PROJ, t)
    return pl.pallas_call(
        _inproj_kernel,
        grid=(t // tm, P_COLS // TN_PROJ),
        in_specs=[
            pl.BlockSpec((tm, D_MODEL), lambda i, j: (i, 0)),
            pl.BlockSpec((1, D_MODEL), lambda i, j: (0, 0)),
            pl.BlockSpec((D_MODEL, TN_PROJ), lambda i, j: (0, j)),
        ],
        out_specs=pl.BlockSpec((tm, TN_PROJ), lambda i, j: (i, j)),
        out_shape=jax.ShapeDtypeStruct((t, P_COLS), F32),
        scratch_shapes=[pltpu.VMEM((tm, D_MODEL), BF16)],
        compiler_params=_params(("arbitrary", "arbitrary")),
        name="inproj",
    )(x2, gain, w)


def _time_block(j, nb, reverse):
    return nb - 1 - j if reverse else j


def _halo_specs(tb, nb, width, col, reverse):
    rows8 = tb // 8
    last8 = nb * rows8 - 1

    def blk(b, j):
        return (b, _time_block(j, nb, reverse), col)

    def prev(b, j):
        return (b, jnp.maximum(_time_block(j, nb, reverse) * rows8 - 1, 0), col)

    def nxt(b, j):
        return (b, jnp.minimum((_time_block(j, nb, reverse) + 1) * rows8, last8), col)

    return [pl.BlockSpec((None, tb, width), blk),
            pl.BlockSpec((None, 8, width), prev),
            pl.BlockSpec((None, 8, width), nxt)]


def _shift_time(x3, halo, back):
    r8 = lax.broadcasted_iota(jnp.int32, (1, 8, 1), 1)
    shift = back % 8
    w = pltpu.roll(x3, shift, 1)
    wh = pltpu.roll(halo, shift, 0)[None]
    if back > 0:
        return jnp.where(r8 < back, jnp.concatenate([wh, w[:-1]], axis=0), w)
    return jnp.where(r8 >= 8 + back, jnp.concatenate([w[1:], wh], axis=0), w)


def _softplus(z):
    return jnp.maximum(z, 0.0) + jnp.log(1.0 + jnp.exp(-jnp.abs(z)))


def _lru_kernel(*refs, reverse, final, tb, nb, seq):
    if final:
        (x_ref, xp_ref, xn_ref, g_ref, hf_ref, cw_ref, cb_ref, wa_ref, wx_ref, ba_ref, bx_ref, lam_ref,
         o_ref, carry_ref) = refs
    else:
        (x_ref, xp_ref, xn_ref, cw_ref, cb_ref, wa_ref, wx_ref, ba_ref, bx_ref, lam_ref,
         o_ref, carry_ref) = refs
    j = pl.program_id(1)
    jj = _time_block(j, nb, reverse)

    @pl.when(j == 0)
    def _():
        carry_ref[...] = jnp.zeros_like(carry_ref)

    ngroups = tb // 8
    width = x_ref.shape[1]
    x3 = x_ref[...].reshape(ngroups, 8, width)
    prev8 = jnp.where(jj > 0, xp_ref[...], 0.0)
    next8 = jnp.where(jj < nb - 1, xn_ref[...], 0.0)
    row = lax.broadcasted_iota(jnp.int32, (tb, 1), 0)
    cw = cw_ref[...]
    xc = (cw[0:1] * _shift_time(x3, prev8, 2) + cw[1:2] * _shift_time(x3, prev8, 1) + cw[2:3] * x3
          + cw[3:4] * _shift_time(x3, next8, -1) + cb_ref[...]).reshape(tb, width)

    xcb = xc.astype(BF16)
    r = jax.nn.sigmoid(_dot(xcb, wa_ref[...]) + ba_ref[...])
    i = jax.nn.sigmoid(_dot(xcb, wx_ref[...]) + bx_ref[...])
    log_a = -LRU_C * r * _softplus(-lam_ref[...])
    a = jnp.exp(log_a)
    om = 1.0 - a * a
    mult = jnp.where(om > 0.0, om * lax.rsqrt(om), 0.0)
    gpos = jj * tb + row
    first = (gpos == seq - 1) if reverse else (gpos == 0)
    mult = jnp.where(first, 1.0, mult)
    u = mult * i * xc

    r8 = lax.broadcasted_iota(jnp.int32, (1, 8, 1), 1)
    acc_a, acc_u = a.reshape(ngroups, 8, width), u.reshape(ngroups, 8, width)
    for s in (1, 2, 4):
        shift, valid = (8 - s, r8 < 8 - s) if reverse else (s, r8 >= s)
        sh_a, sh_u = pltpu.roll(acc_a, shift, 1), pltpu.roll(acc_u, shift, 1)
        acc_u = acc_a * jnp.where(valid, sh_u, 0.0) + acc_u
        acc_a = acc_a * jnp.where(valid, sh_a, 1.0)
    acc_a, acc_u = acc_a.reshape(tb, width), acc_u.reshape(tb, width)
    last_row = 0 if reverse else 7
    carry = carry_ref[...]
    carries = [None] * ngroups
    for gi in (range(ngroups - 1, -1, -1) if reverse else range(ngroups)):
        carries[gi] = carry
        at = 8 * gi + last_row
        carry = (jnp.broadcast_to(acc_a[at:at + 1], carry.shape) * carry
                 + jnp.broadcast_to(acc_u[at:at + 1], carry.shape))
    carry_ref[...] = carry
    h = acc_a * jnp.concatenate(carries, axis=0) + acc_u

    if final:
        o_ref[...] = ((hf_ref[...] + h) * jax.nn.gelu(g_ref[...], approximate=True)).astype(o_ref.dtype)
    else:
        o_ref[...] = h


def _lru(p, hf, wl, d, final):
    bsz, seq, _ = p.shape
    tb = min(TB_LRU, seq)
    nb = seq // tb
    reverse = d == 1
    in_specs = _halo_specs(tb, nb, LRU_WIDTH, P_LRU_X // LRU_WIDTH, reverse)
    args = [p, p, p]
    blk = lambda b, j: (b, _time_block(j, nb, reverse), 0)
    if final:
        in_specs += [pl.BlockSpec((None, tb, LRU_WIDTH), lambda b, j: (b, _time_block(j, nb, reverse), P_LRU_G // LRU_WIDTH)),
                     pl.BlockSpec((None, tb, LRU_WIDTH), blk)]
        args += [p, hf]
    const = lambda shape: pl.BlockSpec(shape, lambda b, j: (0,) * len(shape))
    in_specs += [const((4, LRU_WIDTH)), const((1, LRU_WIDTH)), const((LRU_WIDTH, LRU_WIDTH)),
                 const((LRU_WIDTH, LRU_WIDTH)), const((1, LRU_WIDTH)), const((1, LRU_WIDTH)), const((1, LRU_WIDTH))]
    args += [wl["conv_w"], wl["conv_b"], wl["lru_wa"][d], wl["lru_wx"][d], wl["lru_ba"][d], wl["lru_bx"][d],
             wl["lru_lambda"][d]]
    return pl.pallas_call(
        functools.partial(_lru_kernel, reverse=reverse, final=final, tb=tb, nb=nb, seq=seq),
        grid=(bsz, nb),
        in_specs=in_specs,
        out_specs=pl.BlockSpec((None, tb, LRU_WIDTH), blk),
        out_shape=jax.ShapeDtypeStruct((bsz, seq, LRU_WIDTH), BF16 if final else F32),
        scratch_shapes=[pltpu.VMEM((8, LRU_WIDTH), F32)],
        compiler_params=_params(("arbitrary", "arbitrary")),
        name="lru_bwd" if final else "lru_fwd",
    )(*args)


def _seg_sum(x, ones_bd):
    xb = x.astype(BF16)
    return jnp.concatenate(
        [_dot(xb[:, HALF * hf:HALF * (hf + 1)], ones_bd) for hf in range(x.shape[1] // HALF)], axis=1)


def _rwkv_kernel(*refs, reverse, final, tb, nb, sub, chunk):
    if final:
        (z_ref, zp_ref, zn_ref, yf_ref, tri_ref, tot_ref, ones_ref, hm_ref, bdm_ref, stm_ref,
         mu_ref, w0_ref, w2_ref, a0_ref, a2_ref, kk_ref, ka_ref,
         a0f_ref, a2f_ref, kaf_ref, g2_ref, rk_ref, gnw_ref, gnb_ref,
         o_ref, s_ref, qk_s, rt_s, kt_s, bt_s, ktg_s, btg_s, v_s, rp_s, egl_s, y_s, en_s, f_s) = refs
    else:
        (z_ref, zp_ref, zn_ref, tri_ref, tot_ref, ones_ref, hm_ref, bdm_ref, stm_ref,
         mu_ref, w0_ref, w2_ref, a0_ref, a2_ref, kk_ref, ka_ref,
         o_ref, s_ref, qk_s, rt_s, kt_s, bt_s, ktg_s, btg_s, v_s, rp_s, egl_s, y_s, en_s, f_s) = refs
    j = pl.program_id(1)
    jj = _time_block(j, nb, reverse)
    W = RWKV_WIDTH
    group = HALF // chunk
    halves = W // HALF
    nchunk = sub // chunk
    n_levels = int(math.log2(chunk))
    order = range(nchunk - 1, -1, -1) if reverse else range(nchunk)

    @pl.when(j == 0)
    def _():
        s_ref[...] = jnp.zeros_like(s_ref)

    ones_bd = ones_ref[...]
    tri = tri_ref[...]
    tot = tot_ref[...]
    head_mask = hm_ref[...]
    bd_mask = bdm_ref[...]
    st_mask = stm_ref[...]
    wi = lax.broadcasted_iota(jnp.int32, (chunk, HALF), 0)
    wj = lax.broadcasted_iota(jnp.int32, (chunk, HALF), 1) % chunk
    strict = (wj > wi) if reverse else (wj < wi)
    incl = (wj >= wi) if reverse else (wj <= wi)
    eye_w = jnp.where(wi == wj, 1.0, 0.0)

    def stack(xb):
        return jnp.concatenate([xb] * group, axis=0) * head_mask

    def blockdiag(xw):
        return jnp.concatenate([xw.astype(BF16)] * group, axis=0) * bd_mask

    def lanes(hf):
        return slice(HALF * hf, HALF * hf + HALF)

    saved = {}

    def prep(base):
        blk = slice(base, base + sub)
        z3 = z_ref[blk, :].reshape(sub // 8, 8, RWKV_PAD)
        prev8 = jnp.where(jj > 0, zp_ref[...], 0.0) if base == 0 else z_ref[base - 8:base, :]
        next8 = jnp.where(jj < nb - 1, zn_ref[...], 0.0) if base + sub == tb else z_ref[base + sub:base + sub + 8, :]
        neigh = _shift_time(z3, prev8, 1) + _shift_time(z3, next8, -1)
        yield
        zs = (z3 + (0.5 * neigh - z3) * mu_ref[...]).reshape(sub, RWKV_PAD)
        yield
        r = zs[:, 0:W]
        k = zs[:, W:2 * W]
        v = zs[:, 2 * W:3 * W]
        dw = zs[:, 3 * W:3 * W + 2 * LORA_RANK]
        da = zs[:, 3 * W + 2 * LORA_RANK:3 * W + 4 * LORA_RANK]
        dg = zs[:, 3 * W + 4 * LORA_RANK:3 * W + 4 * LORA_RANK + GATE_RANK]

        lw = -DECAY_SCALE * jax.nn.sigmoid(_dot(jnp.tanh(dw).astype(BF16), w2_ref[...]) + w0_ref[...])
        yield
        dab = da.astype(BF16)
        a = jax.nn.sigmoid(_dot(dab, a2_ref[...]) + a0_ref[...])
        kdir = k * (1.0 + (a - 1.0) * ka_ref[...])
        yield
        kk = k * kk_ref[...]
        kk = kk * lax.rsqrt(jnp.maximum(_seg_sum(kk * kk, ones_bd), 1e-24))
        b = kk * a
        yield

        hi, lo = _split2(lw)
        g = _dot(tri, hi) + _dot(tri, lo)
        gl = _dot(tot, hi) + _dot(tot, lo)
        yield
        e_g = jnp.exp(g)
        e_ng = jnp.exp(-g)
        e_gl = jnp.exp(gl)
        yield
        kt = kdir * e_ng
        bt = b * e_ng
        qk_s[blk, :] = (kk * jnp.exp(g - lw)).astype(BF16)
        rt_s[blk, :] = (r * e_g).astype(BF16)
        yield
        kt_s[blk, :] = kt.astype(BF16)
        bt_s[blk, :] = bt.astype(BF16)
        ktg_s[blk, :] = (kt * e_gl).astype(BF16)
        yield
        btg_s[blk, :] = (-(bt * e_gl)).astype(BF16)
        v_s[blk, :] = v.astype(BF16)
        egl_s[blk, :] = e_gl
        if final:
            yield
            a_f = jax.nn.sigmoid(_dot(dab, a2f_ref[...]) + a0f_ref[...])
            kdir_f = k * (1.0 + (a_f - 1.0) * kaf_ref[...])
            bonus = _seg_sum(r * (kdir_f + kdir) * rk_ref[...], ones_bd) * v
            yield
            gate = _dot(jax.nn.sigmoid(dg).astype(BF16), g2_ref[...])
            saved[base] = (bonus, gate)

    def chains(base):
        ids = [(c, hf) for c in order for hf in range(halves)]
        n = len(ids)

        def rows(c):
            return slice(base + c * chunk, base + (c + 1) * chunk)

        def slot(c, hf):
            return (base // chunk + c) * halves + hf

        a_k, b_k, b_b, pw, inv = [], [], [], [], []
        for c, hf in ids:
            sl, ls = rows(c), lanes(hf)
            x2 = jnp.concatenate([qk_s[sl, ls], rt_s[sl, ls]], axis=0)
            gk = _dot_nt(x2, stack(kt_s[sl, ls]))
            gb = _dot_nt(x2, stack(bt_s[sl, ls]))
            a_k.append(jnp.where(strict, gk[:chunk], 0.0).astype(BF16))
            b_k.append(jnp.where(incl, gk[chunk:], 0.0).astype(BF16))
            b_b.append(jnp.where(incl, gb[chunk:], 0.0).astype(BF16))
            inv.append(jnp.where(strict, -gb[:chunk], 0.0))
            yield
        for i in range(n):
            pw.append(_dot(inv[i].astype(BF16), blockdiag(inv[i])))
            inv[i] = eye_w + inv[i]
            yield
        for level in range(1, n_levels):
            for i in range(n):
                bdp = blockdiag(pw[i])
                if level < n_levels - 1:
                    both = _dot(jnp.concatenate([pw[i], inv[i]], axis=0).astype(BF16), bdp)
                    pw[i] = both[:chunk]
                    inv[i] = inv[i] + both[chunk:]
                else:
                    inv[i] = inv[i] + _dot(inv[i].astype(BF16), bdp)
                yield
        tw = [t.astype(BF16) for t in inv]
        vm, akv, qp, vp = [], [], [], []
        for i, (c, hf) in enumerate(ids):
            vm.append(stack(v_s[rows(c), lanes(hf)]))
            akv.append(_dot(a_k[i], vm[i]))
            yield
        for i, (c, hf) in enumerate(ids):
            qp.append(_dot(tw[i], stack(qk_s[rows(c), lanes(hf)])).astype(BF16))
            yield
        for i in range(n):
            vp.append(_dot(tw[i], stack(akv[i].astype(BF16))).astype(BF16))
            yield
        for i, (c, hf) in enumerate(ids):
            sl, ls = rows(c), lanes(hf)
            rp_s[sl, ls] = (rt_s[sl, ls].astype(F32) - _dot(b_b[i], stack(qp[i]))).astype(BF16)
            yield
        for i, (c, hf) in enumerate(ids):
            y_s[rows(c), lanes(hf)] = _dot(b_k[i], vm[i]) - _dot(b_b[i], stack(vp[i]))
            yield
        for i, (c, hf) in enumerate(ids):
            en = _dot_tn(qp[i], btg_s[rows(c), lanes(hf)])
            en_s[slot(c, hf)] = (en * st_mask).astype(BF16)
            yield
        for i, (c, hf) in enumerate(ids):
            sl, ls = rows(c), lanes(hf)
            fm = _dot_tn(jnp.concatenate([v_s[sl, ls], vp[i]], axis=0),
                         jnp.concatenate([ktg_s[sl, ls], btg_s[sl, ls]], axis=0))
            f_s[slot(c, hf)] = fm * st_mask
            yield

    def walk(base):
        for c in order:
            sl = slice(base + c * chunk, base + (c + 1) * chunk)
            for hf in range(halves):
                ls = lanes(hf)
                sidx = (base // chunk + c) * halves + hf
                st = s_ref[hf]
                stb = st.astype(BF16)
                y_s[sl, ls] = y_s[sl, ls] + _dot_nt(rp_s[sl, ls], stb)
                decay = egl_s[sl, ls][0:1]
                s_ref[hf] = st * decay + _dot(stb, en_s[sidx]) + f_s[sidx]
            yield

    def tail(base):
        blk = slice(base, base + sub)
        if not final:
            o_ref[blk, :] = y_s[blk, :]
            return
        bonus, gate = saved.pop(base)
        y = yf_ref[blk, :] + y_s[blk, :]
        inv_n = 1.0 / RWKV_HEAD
        mean = _seg_sum(y, ones_bd) * inv_n
        yield
        yc = y - mean
        var = _seg_sum(yc * yc, ones_bd) * inv_n
        yield
        yn = yc * lax.rsqrt(var + GN_EPS) * gnw_ref[...] + gnb_ref[...]
        o_ref[blk, :] = ((yn + bonus) * gate).astype(o_ref.dtype)

    def walk_then_tail(base):
        yield from walk(base)
        yield from tail(base)

    def drain(gen):
        for _ in gen:
            pass

    def weave(main, sides, stride):
        sides, finished = list(sides), object()
        for step, _ in enumerate(main):
            if step % stride == 0:
                sides = [g for g in sides if next(g, finished) is not finished]
        for g in sides:
            drain(g)

    nsub = tb // sub
    bases = [sb * sub for sb in (range(nsub - 1, -1, -1) if reverse else range(nsub))]
    chain_steps = (n_levels + 8) * nchunk * halves
    drain(prep(bases[0]))
    for idx, base in enumerate(bases):
        sides = []
        if idx > 0:
            sides.append(walk_then_tail(bases[idx - 1]))
        if idx + 1 < nsub:
            sides.append(prep(bases[idx + 1]))
        weave(chains(base), sides, max(1, chain_steps // 16))
    drain(walk_then_tail(bases[-1]))


def _rwkv_masks(tb, chunk, reverse):
    t = np.arange(tb)
    same = (t[:, None] // chunk) == (t[None, :] // chunk)
    before = (t[None, :] >= t[:, None]) if reverse else (t[None, :] <= t[:, None])
    h = np.arange(HALF)

    def bd(row_block, col_block):
        return (h[:, None] // row_block) == (h[None, :] // col_block)

    bf = lambda m: m.astype(np.float32).astype(jnp.bfloat16)
    return [bf(same & before), bf(same), bf(bd(RWKV_HEAD, RWKV_HEAD)), bf(bd(chunk, RWKV_HEAD)),
            bf(bd(chunk, chunk)), bd(RWKV_HEAD, RWKV_HEAD).astype(np.float32)]


def _rwkv(p, yf, wl, d, final):
    bsz, seq, _ = p.shape
    tb = min(TB_RWKV, seq)
    nb = seq // tb
    reverse = d == 1
    W = RWKV_WIDTH
    in_specs = _halo_specs(tb, nb, RWKV_PAD, 0, reverse)
    args = [p, p, p]
    blk = lambda b, j: (b, _time_block(j, nb, reverse), 0)
    const = lambda shape: pl.BlockSpec(shape, lambda b, j: (0,) * len(shape))
    if final:
        in_specs += [pl.BlockSpec((None, tb, W), blk)]
        args += [yf]
    sub = min(SUB_RWKV, tb)
    masks = _rwkv_masks(sub, CHUNK, reverse)
    in_specs += [const(m.shape) for m in masks]
    args += [jnp.asarray(m) for m in masks]
    in_specs += [const((1, RWKV_PAD)), const((1, W)), const((2 * LORA_RANK, W)), const((1, W)),
                 const((2 * LORA_RANK, W)), const((1, W)), const((1, W))]
    args += [wl["rwkv_mu"], wl["rwkv_w0"][d], wl["rwkv_w2"][d], wl["rwkv_a0"][d], wl["rwkv_a2"][d],
             wl["rwkv_k_k"], wl["rwkv_k_a"][d]]
    if final:
        in_specs += [const((1, W)), const((2 * LORA_RANK, W)), const((1, W)), const((GATE_RANK, W)),
                     const((1, W)), const((1, W)), const((1, W))]
        args += [wl["rwkv_a0"][0], wl["rwkv_a2"][0], wl["rwkv_k_a"][0], wl["rwkv_g2"], wl["rwkv_r_k"],
                 wl["rwkv_gn_w"], wl["rwkv_gn_b"]]
    scratch = [pltpu.VMEM((W // HALF, HALF, HALF), F32)]
    scratch += [pltpu.VMEM((tb, W), BF16)] * 8
    scratch += [pltpu.VMEM((tb, W), F32)] * 2
    nhc = (tb // CHUNK) * (W // HALF)
    scratch += [pltpu.VMEM((nhc, HALF, HALF), BF16), pltpu.VMEM((nhc, HALF, HALF), F32)]
    return pl.pallas_call(
        functools.partial(_rwkv_kernel, reverse=reverse, final=final, tb=tb, nb=nb, sub=sub, chunk=CHUNK),
        grid=(bsz, nb),
        in_specs=in_specs,
        out_specs=pl.BlockSpec((None, tb, W), blk),
        out_shape=jax.ShapeDtypeStruct((bsz, seq, W), BF16 if final else F32),
        scratch_shapes=scratch,
        compiler_params=_params(("arbitrary", "arbitrary")),
        name="rwkv_bwd" if final else "rwkv_fwd",
    )(*args)


def _t5_bucket_table():
    qi = np.arange(ATTN_BLOCK)[:, None]
    kj = np.arange(3 * ATTN_BLOCK)[None, :]
    rel = kj - ATTN_BLOCK - qi
    half = N_BUCKETS // 2
    max_exact = half // 2
    ret = np.where(rel > 0, half, 0)
    n = np.abs(rel)
    nf = np.maximum(n, 1).astype(np.float32)
    large = max_exact + (np.log(nf / max_exact) / math.log(MAX_DISTANCE / max_exact)
                         * (half - max_exact)).astype(np.int32)
    large = np.minimum(large, half - 1)
    return (ret + np.where(n < max_exact, n, large)).astype(np.int32)


def _attn_kernel(q_ref, kp_ref, kc_ref, kn_ref, vp_ref, vc_ref, vn_ref, bkt_ref, rb_ref, sink_ref,
                 o_ref, bias_ref, *, nb):
    b = pl.program_id(0)
    j = pl.program_id(1)
    blk = ATTN_BLOCK

    @pl.when((b == 0) & (j == 0))
    def _():
        bucket = bkt_ref[...]
        qi = lax.broadcasted_iota(jnp.int32, (blk, 3 * blk), 0)
        kj = lax.broadcasted_iota(jnp.int32, (blk, 3 * blk), 1)
        in_window = jnp.abs(kj - blk - qi) <= WINDOW
        for h in range(N_Q_HEADS):
            def body(t, acc, h=h):
                return jnp.where(bucket == t, rb_ref[t, h], acc)
            bias = lax.fori_loop(0, N_BUCKETS, body, jnp.zeros((blk, 3 * blk), F32))
            bias_ref[h] = jnp.where(in_window, bias, NEG)

    kcol = lax.broadcasted_iota(jnp.int32, (1, 3 * blk), 1)
    off_seq = ((kcol < blk) & (j == 0)) | ((kcol >= 2 * blk) & (j == nb - 1))
    hrow = lax.broadcasted_iota(jnp.int32, (GQA_GROUP * blk, 1), 0) // blk
    scale = HEAD_DIM ** -0.5
    kcat = jnp.concatenate([kp_ref[...], kc_ref[...], kn_ref[...]], axis=0).astype(BF16)
    vcat = jnp.concatenate([vp_ref[...], vc_ref[...], vn_ref[...]], axis=0).astype(BF16)
    for g in range(N_KV_HEADS):
        qs = jnp.concatenate(
            [q_ref[:, HEAD_DIM * (GQA_GROUP * g + hh):HEAD_DIM * (GQA_GROUP * g + hh + 1)] for hh in range(GQA_GROUP)],
            axis=0).astype(BF16)
        kg = kcat[:, HEAD_DIM * g:HEAD_DIM * (g + 1)]
        vg = vcat[:, HEAD_DIM * g:HEAD_DIM * (g + 1)]
        bias = bias_ref[GQA_GROUP * g:GQA_GROUP * (g + 1)].reshape(GQA_GROUP * blk, 3 * blk)
        s = jnp.where(off_seq, NEG, _dot_nt(qs, kg) * scale + bias)
        sk = jnp.zeros((GQA_GROUP * blk, 1), F32)
        for hh in range(GQA_GROUP):
            sk = jnp.where(hrow == hh, sink_ref[0, GQA_GROUP * g + hh], sk)
        m = jnp.maximum(jnp.max(s, axis=-1, keepdims=True), sk)
        pexp = jnp.exp(s - m)
        den = jnp.sum(pexp, axis=-1, keepdims=True) + jnp.exp(sk - m)
        o = _dot(pexp.astype(BF16), vg) / den
        for hh in range(GQA_GROUP):
            h = GQA_GROUP * g + hh
            o_ref[:, HEAD_DIM * h:HEAD_DIM * (h + 1)] = o[blk * hh:blk * (hh + 1)].astype(o_ref.dtype)


def _attn(p, wl):
    bsz, seq, _ = p.shape
    blk = ATTN_BLOCK
    nb = seq // blk
    kcol, vcol = P_K // KV_WIDTH, P_V // KV_WIDTH

    def kv(col, off):
        return pl.BlockSpec((None, blk, KV_WIDTH), lambda b, j: (b, jnp.clip(j + off, 0, nb - 1), col))

    smem = lambda: pl.BlockSpec(memory_space=pltpu.SMEM)
    return pl.pallas_call(
        functools.partial(_attn_kernel, nb=nb),
        grid=(bsz, nb),
        in_specs=[pl.BlockSpec((None, blk, ATTN_WIDTH), lambda b, j: (b, j, P_Q // ATTN_WIDTH)),
                  kv(kcol, -1), kv(kcol, 0), kv(kcol, 1), kv(vcol, -1), kv(vcol, 0), kv(vcol, 1),
                  pl.BlockSpec((blk, 3 * blk), lambda b, j: (0, 0)), smem(), smem()],
        out_specs=pl.BlockSpec((None, blk, ATTN_WIDTH), lambda b, j: (b, j, 0)),
        out_shape=jax.ShapeDtypeStruct((bsz, seq, ATTN_WIDTH), BF16),
        scratch_shapes=[pltpu.VMEM((N_Q_HEADS, blk, 3 * blk), F32)],
        compiler_params=_params(("arbitrary", "arbitrary")),
        name="attn",
    )(p, p, p, p, p, p, p, wl["bucket"], wl["rel_bias"], wl["attn_sink"])


def _outproj_kernel(lru_ref, rwkv_ref, attn_ref, x_ref, w_ref, g_ref, o_ref):
    mix = (_dot(lru_ref[...], w_ref[0:LRU_WIDTH, :])
           + _dot(rwkv_ref[...], w_ref[LRU_WIDTH:LRU_WIDTH + RWKV_WIDTH, :])
           + _dot(attn_ref[...], w_ref[LRU_WIDTH + RWKV_WIDTH:, :]))
    ms = jnp.mean(mix * mix, axis=-1, keepdims=True)
    o_ref[...] = x_ref[...] + mix * lax.rsqrt(ms + EPS) * g_ref[...]


def _outproj(lru, rwkv, attn, x2, w, gain):
    t = x2.shape[0]
    tm = min(TM_OUT, t)
    row = lambda width: pl.BlockSpec((tm, width), lambda i: (i, 0))
    return pl.pallas_call(
        _outproj_kernel,
        grid=(t // tm,),
        in_specs=[row(LRU_WIDTH), row(RWKV_WIDTH), row(ATTN_WIDTH), row(D_MODEL),
                  pl.BlockSpec((D_MODEL, D_MODEL), lambda i: (0, 0)),
                  pl.BlockSpec((1, D_MODEL), lambda i: (0, 0))],
        out_specs=row(D_MODEL),
        out_shape=jax.ShapeDtypeStruct((t, D_MODEL), F32),
        compiler_params=_params(("arbitrary",)),
        name="outproj",
    )(lru, rwkv, attn, x2, w, gain)


def _ffn_kernel(x_ref, gpre_ref, wu_ref, wd_ref, gpost_ref, o_ref, h_ref, acc_ref):
    f = pl.program_id(1)

    @pl.when(f == 0)
    def _():
        x = x_ref[...]
        ms = jnp.mean(x * x, axis=-1, keepdims=True)
        h_ref[...] = (x * lax.rsqrt(ms + EPS) * gpre_ref[...]).astype(BF16)
        acc_ref[...] = jnp.zeros_like(acc_ref)

    u = jnp.maximum(_dot(h_ref[...], wu_ref[...]), 0.0)
    acc_ref[...] += _dot((u * u).astype(BF16), wd_ref[...])

    @pl.when(f == pl.num_programs(1) - 1)
    def _():
        y = acc_ref[...]
        ms = jnp.mean(y * y, axis=-1, keepdims=True)
        o_ref[...] = x_ref[...] + y * lax.rsqrt(ms + EPS) * gpost_ref[...]


def _ffn(x2, gpre, wu, wd, gpost):
    t = x2.shape[0]
    tm = min(TM_FFN, t)
    return pl.pallas_call(
        _ffn_kernel,
        grid=(t // tm, D_FF // TF_FFN),
        in_specs=[pl.BlockSpec((tm, D_MODEL), lambda i, f: (i, 0)),
                  pl.BlockSpec((1, D_MODEL), lambda i, f: (0, 0)),
                  pl.BlockSpec((D_MODEL, TF_FFN), lambda i, f: (0, f)),
                  pl.BlockSpec((TF_FFN, D_MODEL), lambda i, f: (f, 0)),
                  pl.BlockSpec((1, D_MODEL), lambda i, f: (0, 0))],
        out_specs=pl.BlockSpec((tm, D_MODEL), lambda i, f: (i, 0)),
        out_shape=jax.ShapeDtypeStruct((t, D_MODEL), F32),
        scratch_shapes=[pltpu.VMEM((tm, D_MODEL), BF16), pltpu.VMEM((tm, D_MODEL), F32)],
        compiler_params=_params(("arbitrary", "arbitrary")),
        name="ffn",
    )(x2, gpre, wu, wd, gpost)


def _block_diag(w):
    n, bi, bj = w.shape
    eye = jnp.eye(n, dtype=w.dtype)
    return (eye[:, None, :, None] * w[:, :, None, :]).reshape(n * bi, n * bj)


def _pad_lora(w2, d):
    z = jnp.zeros_like(w2[d])
    return jnp.concatenate([w2[0], z] if d == 0 else [z, w2[1]], axis=0)


def _layer_weights(l, w):
    row = lambda v: v.reshape(1, -1).astype(F32)
    w_in = w["w_in"][l]
    w_in_p = jnp.concatenate(
        [w_in[:, OFF_RWKV:OFF_ATTN], jnp.zeros((D_MODEL, RWKV_PAD - RWKV_COLS), w_in.dtype),
         w_in[:, 0:OFF_RWKV], w_in[:, OFF_ATTN:]], axis=1).astype(BF16)
    mu = jnp.concatenate([w["rwkv_mu"][l], jnp.zeros((RWKV_PAD - RWKV_COLS,), F32)]).reshape(1, -1)
    return {
        "norm_mix_pre": row(w["norm_mix_pre"][l]), "norm_mix_post": row(w["norm_mix_post"][l]),
        "norm_ffn_pre": row(w["norm_ffn_pre"][l]), "norm_ffn_post": row(w["norm_ffn_post"][l]),
        "w_in": w_in_p, "w_out": w["w_out"][l].astype(BF16),
        "conv_w": w["conv_w"][l].astype(F32), "conv_b": row(w["conv_b"][l]),
        "lru_wa": [_block_diag(w["lru_wa"][l, d]).astype(BF16) for d in range(2)],
        "lru_wx": [_block_diag(w["lru_wx"][l, d]).astype(BF16) for d in range(2)],
        "lru_ba": [row(w["lru_ba"][l, d]) for d in range(2)],
        "lru_bx": [row(w["lru_bx"][l, d]) for d in range(2)],
        "lru_lambda": [row(w["lru_lambda"][l, d]) for d in range(2)],
        "rwkv_mu": mu,
        "rwkv_w0": [row(w["rwkv_w0"][l, d]) for d in range(2)],
        "rwkv_w2": [_pad_lora(w["rwkv_w2"][l], d).astype(BF16) for d in range(2)],
        "rwkv_a0": [row(w["rwkv_a0"][l, d]) for d in range(2)],
        "rwkv_a2": [_pad_lora(w["rwkv_a2"][l], d).astype(BF16) for d in range(2)],
        "rwkv_g2": w["rwkv_g2"][l].astype(BF16),
        "rwkv_k_k": row(w["rwkv_k_k"][l]),
        "rwkv_k_a": [row(w["rwkv_k_a"][l, d]) for d in range(2)],
        "rwkv_r_k": row(w["rwkv_r_k"][l]),
        "rwkv_gn_w": row(w["rwkv_gn_w"][l]), "rwkv_gn_b": row(w["rwkv_gn_b"][l]),
        "attn_sink": w["attn_sink"][l].reshape(1, -1).astype(F32),
        "rel_bias": w["rel_bias"].astype(F32),
        "bucket": jnp.asarray(_t5_bucket_table()),
        "w_up": w["w_up"][l].astype(BF16), "w_down": w["w_down"][l].astype(BF16),
    }


def _trunk(x, layers):
    bsz, seq, _ = x.shape
    t = bsz * seq
    x2 = x.reshape(t, D_MODEL)
    for wl in layers:
        p = _inproj(x2, wl["norm_mix_pre"], wl["w_in"]).reshape(bsz, seq, P_COLS)
        hf = _lru(p, None, wl, 0, False)
        lru_out = _lru(p, hf, wl, 1, True)
        yf = _rwkv(p, None, wl, 0, False)
        rwkv_out = _rwkv(p, yf, wl, 1, True)
        attn_out = _attn(p, wl)
        x2 = _outproj(lru_out.reshape(t, LRU_WIDTH), rwkv_out.reshape(t, RWKV_WIDTH),
                      attn_out.reshape(t, ATTN_WIDTH), x2, wl["w_out"], wl["norm_mix_post"])
        x2 = _ffn(x2, wl["norm_ffn_pre"], wl["w_up"], wl["w_down"], wl["norm_ffn_post"])
    return x2.reshape(bsz, seq, D_MODEL)


def kernel(x_prompt, x_sample, norm_mix_pre, norm_mix_post, norm_ffn_pre, norm_ffn_post, w_in, w_out, conv_w, conv_b, lru_wa, lru_ba, lru_wx, lru_bx, lru_lambda, rwkv_mu, rwkv_w0, rwkv_w2, rwkv_a0, rwkv_a2, rwkv_g2, rwkv_k_k, rwkv_k_a, rwkv_r_k, rwkv_gn_w, rwkv_gn_b, attn_sink, rel_bias, w_up, w_down):
    w = dict(norm_mix_pre=norm_mix_pre, norm_mix_post=norm_mix_post, norm_ffn_pre=norm_ffn_pre,
             norm_ffn_post=norm_ffn_post, w_in=w_in, w_out=w_out, conv_w=conv_w, conv_b=conv_b,
             lru_wa=lru_wa, lru_ba=lru_ba, lru_wx=lru_wx, lru_bx=lru_bx, lru_lambda=lru_lambda,
             rwkv_mu=rwkv_mu, rwkv_w0=rwkv_w0, rwkv_w2=rwkv_w2, rwkv_a0=rwkv_a0, rwkv_a2=rwkv_a2,
             rwkv_g2=rwkv_g2, rwkv_k_k=rwkv_k_k, rwkv_k_a=rwkv_k_a, rwkv_r_k=rwkv_r_k,
             rwkv_gn_w=rwkv_gn_w, rwkv_gn_b=rwkv_gn_b, attn_sink=attn_sink, rel_bias=rel_bias,
             w_up=w_up, w_down=w_down)
    layers = [_layer_weights(l, w) for l in range(w_in.shape[0])]
    return (_trunk(x_prompt, layers), _trunk(x_sample, layers))
```

```python
import functools
import math

import numpy as np
import jax
import jax.numpy as jnp
from jax import lax
from jax.experimental import pallas as pl
from jax.experimental.pallas import tpu as pltpu

F32 = jnp.float32
BF16 = jnp.bfloat16

D_MODEL = 2048
LRU_WIDTH = 512
LRU_BLOCKS = 8
LRU_C = 8.0
RWKV_WIDTH = 512
RWKV_HEAD = 64
RWKV_HEADS = 8
LORA_RANK = 64
GATE_RANK = 128
DECAY_SCALE = 0.606531
GN_EPS = 64e-5
ATTN_WIDTH = 1024
HEAD_DIM = 128
N_Q_HEADS = 8
N_KV_HEADS = 2
GQA_GROUP = 4
KV_WIDTH = 256
WINDOW = 128
ATTN_BLOCK = 128
N_BUCKETS = 32
MAX_DISTANCE = 128
D_FF = 8192
EPS = 1e-6
OFF_RWKV = 2 * LRU_WIDTH
RWKV_COLS = 3 * RWKV_WIDTH + 4 * LORA_RANK + GATE_RANK
OFF_ATTN = OFF_RWKV + RWKV_COLS

RWKV_PAD = 2048
P_LRU_X = 2048
P_LRU_G = 2560
P_Q = 3072
P_K = 4096
P_V = 4352
P_COLS = 4608

LANES = 128
HALF = 256
VMEM_LIMIT = 56 * 1024 * 1024

TM_PROJ = 1024
TN_PROJ = 1536
TM_OUT = 512
TM_FFN = 512
TF_FFN = 1024
TB_LRU = 256
TB_RWKV = 512
SUB_RWKV = 256
CHUNK = 64
NEG = -1e30


def _dot(a, b):
    return jnp.dot(a, b, preferred_element_type=F32)


def _dot_nt(a, b):
    return lax.dot_general(a, b, (((1,), (1,)), ((), ())), preferred_element_type=F32)


def _dot_tn(a, b):
    return lax.dot_general(a, b, (((0,), (0,)), ((), ())), preferred_element_type=F32)


def _split2(x):
    hi = x.astype(BF16)
    lo = (x - hi.astype(F32)).astype(BF16)
    return hi, lo


def _params(sem):
    return pltpu.CompilerParams(dimension_semantics=sem, vmem_limit_bytes=VMEM_LIMIT)


def _inproj_kernel(x_ref, g_ref, w_ref, o_ref, h_ref):
    @pl.when(pl.program_id(1) == 0)
    def _():
        x = x_ref[...]
        ms = jnp.mean(x * x, axis=-1, keepdims=True)
        h_ref[...] = (x * lax.rsqrt(ms + EPS) * g_ref[...]).astype(BF16)

    o_ref[...] = _dot(h_ref[...], w_ref[...])


def _inproj(x2, gain, w):
    t = x2.shape[0]
    tm = min(TM_PROJ, t)
    return pl.pallas_call(
        _inproj_kernel,
        grid=(t // tm, P_COLS // TN_PROJ),
        in_specs=[
            pl.BlockSpec((tm, D_MODEL), lambda i, j: (i, 0)),
            pl.BlockSpec((1, D_MODEL), lambda i, j: (0, 0)),
            pl.BlockSpec((D_MODEL, TN_PROJ), lambda i, j: (0, j)),
        ],
        out_specs=pl.BlockSpec((tm, TN_PROJ), lambda i, j: (i, j)),
        out_shape=jax.ShapeDtypeStruct((t, P_COLS), F32),
        scratch_shapes=[pltpu.VMEM((tm, D_MODEL), BF16)],
        compiler_params=_params(("arbitrary", "arbitrary")),
        name="inproj",
    )(x2, gain, w)


def _time_block(j, nb, reverse):
    return nb - 1 - j if reverse else j


def _halo_specs(tb, nb, width, col, reverse):
    rows8 = tb // 8
    last8 = nb * rows8 - 1

    def blk(b, j):
        return (b, _time_block(j, nb, reverse), col)

    def prev(b, j):
        return (b, jnp.maximum(_time_block(j, nb, reverse) * rows8 - 1, 0), col)

    def nxt(b, j):
        return (b, jnp.minimum((_time_block(j, nb, reverse) + 1) * rows8, last8), col)

    return [pl.BlockSpec((None, tb, width), blk),
            pl.BlockSpec((None, 8, width), prev),
            pl.BlockSpec((None, 8, width), nxt)]


def _shift_time(x3, halo, back):
    r8 = lax.broadcasted_iota(jnp.int32, (1, 8, 1), 1)
    shift = back % 8
    w = pltpu.roll(x3, shift, 1)
    wh = pltpu.roll(halo, shift, 0)[None]
    if back > 0:
        return jnp.where(r8 < back, jnp.concatenate([wh, w[:-1]], axis=0), w)
    return jnp.where(r8 >= 8 + back, jnp.concatenate([w[1:], wh], axis=0), w)


def _softplus(z):
    return jnp.maximum(z, 0.0) + jnp.log(1.0 + jnp.exp(-jnp.abs(z)))


def _lru_kernel(*refs, reverse, final, tb, nb, seq):
    if final:
        (x_ref, xp_ref, xn_ref, g_ref, hf_ref, cw_ref, cb_ref, wa_ref, wx_ref, ba_ref, bx_ref, lam_ref,
         o_ref, carry_ref) = refs
    else:
        (x_ref, xp_ref, xn_ref, cw_ref, cb_ref, wa_ref, wx_ref, ba_ref, bx_ref, lam_ref,
         o_ref, carry_ref) = refs
    j = pl.program_id(1)
    jj = _time_block(j, nb, reverse)

    @pl.when(j == 0)
    def _():
        carry_ref[...] = jnp.zeros_like(carry_ref)

    ngroups = tb // 8
    width = x_ref.shape[1]
    x3 = x_ref[...].reshape(ngroups, 8, width)
    prev8 = jnp.where(jj > 0, xp_ref[...], 0.0)
    next8 = jnp.where(jj < nb - 1, xn_ref[...], 0.0)
    row = lax.broadcasted_iota(jnp.int32, (tb, 1), 0)
    cw = cw_ref[...]
    xc = (cw[0:1] * _shift_time(x3, prev8, 2) + cw[1:2] * _shift_time(x3, prev8, 1) + cw[2:3] * x3
          + cw[3:4] * _shift_time(x3, next8, -1) + cb_ref[...]).reshape(tb, width)

    xcb = xc.astype(BF16)
    r = jax.nn.sigmoid(_dot(xcb, wa_ref[...]) + ba_ref[...])
    i = jax.nn.sigmoid(_dot(xcb, wx_ref[...]) + bx_ref[...])
    a = jnp.exp2(r * ((-LRU_C * math.log2(math.e)) * _softplus(-lam_ref[...])))
    om = 1.0 - a * a
    mult = jnp.where(om > 0.0, om * lax.rsqrt(om), 0.0)
    gpos = jj * tb + row
    first = (gpos == seq - 1) if reverse else (gpos == 0)
    mult = jnp.where(first, 1.0, mult)
    u = mult * i * xc

    r8 = lax.broadcasted_iota(jnp.int32, (1, 8, 1), 1)
    acc_a, acc_u = a.reshape(ngroups, 8, width), u.reshape(ngroups, 8, width)
    for s in (1, 2, 4):
        shift, valid = (8 - s, r8 < 8 - s) if reverse else (s, r8 >= s)
        sh_a, sh_u = pltpu.roll(acc_a, shift, 1), pltpu.roll(acc_u, shift, 1)
        acc_u = acc_a * jnp.where(valid, sh_u, 0.0) + acc_u
        acc_a = acc_a * jnp.where(valid, sh_a, 1.0)
    acc_a, acc_u = acc_a.reshape(tb, width), acc_u.reshape(tb, width)
    last_row = 0 if reverse else 7
    carry = carry_ref[...]
    carries = [None] * ngroups
    for gi in (range(ngroups - 1, -1, -1) if reverse else range(ngroups)):
        carries[gi] = carry
        at = 8 * gi + last_row
        carry = (jnp.broadcast_to(acc_a[at:at + 1], carry.shape) * carry
                 + jnp.broadcast_to(acc_u[at:at + 1], carry.shape))
    carry_ref[...] = carry
    h = acc_a * jnp.concatenate(carries, axis=0) + acc_u

    if final:
        o_ref[...] = ((hf_ref[...] + h) * jax.nn.gelu(g_ref[...], approximate=True)).astype(o_ref.dtype)
    else:
        o_ref[...] = h


def _lru(p, hf, wl, d, final):
    bsz, seq, _ = p.shape
    tb = min(TB_LRU, seq)
    nb = seq // tb
    reverse = d == 1
    in_specs = _halo_specs(tb, nb, LRU_WIDTH, P_LRU_X // LRU_WIDTH, reverse)
    args = [p, p, p]
    blk = lambda b, j: (b, _time_block(j, nb, reverse), 0)
    if final:
        in_specs += [pl.BlockSpec((None, tb, LRU_WIDTH), lambda b, j: (b, _time_block(j, nb, reverse), P_LRU_G // LRU_WIDTH)),
                     pl.BlockSpec((None, tb, LRU_WIDTH), blk)]
        args += [p, hf]
    const = lambda shape: pl.BlockSpec(shape, lambda b, j: (0,) * len(shape))
    in_specs += [const((4, LRU_WIDTH)), const((1, LRU_WIDTH)), const((LRU_WIDTH, LRU_WIDTH)),
                 const((LRU_WIDTH, LRU_WIDTH)), const((1, LRU_WIDTH)), const((1, LRU_WIDTH)), const((1, LRU_WIDTH))]
    args += [wl["conv_w"], wl["conv_b"], wl["lru_wa"][d], wl["lru_wx"][d], wl["lru_ba"][d], wl["lru_bx"][d],
             wl["lru_lambda"][d]]
    return pl.pallas_call(
        functools.partial(_lru_kernel, reverse=reverse, final=final, tb=tb, nb=nb, seq=seq),
        grid=(bsz, nb),
        in_specs=in_specs,
        out_specs=pl.BlockSpec((None, tb, LRU_WIDTH), blk),
        out_shape=jax.ShapeDtypeStruct((bsz, seq, LRU_WIDTH), BF16 if final else F32),
        scratch_shapes=[pltpu.VMEM((8, LRU_WIDTH), F32)],
        compiler_params=_params(("arbitrary", "arbitrary")),
        name="lru_bwd" if final else "lru_fwd",
    )(*args)


def _seg_sum(x, ones_bd):
    xb = x.astype(BF16)
    return jnp.concatenate(
        [_dot(xb[:, HALF * hf:HALF * (hf + 1)], ones_bd) for hf in range(x.shape[1] // HALF)], axis=1)


def _zero_of(x):
    bits = pltpu.bitcast(x, jnp.uint32)
    return ((bits >> 16) >> 16).astype(F32)


def _rwkv_kernel(*refs, reverse, final, tb, nb, sub, chunk):
    if final:
        (z_ref, zp_ref, zn_ref, yf_ref, tri_ref, ones_ref, hm_ref, bdm_ref, stm_ref,
         mu_ref, w0_ref, w2_ref, a0_ref, a2_ref, kk_ref, ka_ref,
         a0f_ref, a2f_ref, kaf_ref, g2_ref, rk_ref, gnw_ref, gnb_ref,
         o_ref, s_ref, qk_s, rt_s, kt_s, bt_s, ktg_s, btg_s, v_s, rp_s, egl_s, y_s, en_s, f_s) = refs
    else:
        (z_ref, zp_ref, zn_ref, tri_ref, ones_ref, hm_ref, bdm_ref, stm_ref,
         mu_ref, w0_ref, w2_ref, a0_ref, a2_ref, kk_ref, ka_ref,
         o_ref, s_ref, qk_s, rt_s, kt_s, bt_s, ktg_s, btg_s, v_s, rp_s, egl_s, y_s, en_s, f_s) = refs
    j = pl.program_id(1)
    jj = _time_block(j, nb, reverse)
    W = RWKV_WIDTH
    group = HALF // chunk
    halves = W // HALF
    nchunk = sub // chunk
    n_levels = int(math.log2(chunk))
    order = range(nchunk - 1, -1, -1) if reverse else range(nchunk)

    @pl.when(j == 0)
    def _():
        s_ref[...] = jnp.zeros_like(s_ref)

    ones_bd = ones_ref[...]
    tri = tri_ref[...]
    head_mask = hm_ref[...]
    bd_mask = bdm_ref[...]
    st_mask = stm_ref[...]
    wi = lax.broadcasted_iota(jnp.int32, (chunk, HALF), 0)
    wj = lax.broadcasted_iota(jnp.int32, (chunk, HALF), 1) % chunk
    strict = (wj > wi) if reverse else (wj < wi)
    incl = (wj >= wi) if reverse else (wj <= wi)
    eye_w = jnp.where(wi == wj, 1.0, 0.0)

    def stack(xb):
        return jnp.concatenate([xb] * group, axis=0) * head_mask

    def blockdiag(xw):
        return jnp.concatenate([xw.astype(BF16)] * group, axis=0) * bd_mask

    def lanes(hf):
        return slice(HALF * hf, HALF * hf + HALF)

    saved = {}
    tokens = []

    def prep(base):
        blk = slice(base, base + sub)
        z3 = z_ref[blk, :].reshape(sub // 8, 8, RWKV_PAD)
        prev8 = jnp.where(jj > 0, zp_ref[...], 0.0) if base == 0 else z_ref[base - 8:base, :]
        next8 = jnp.where(jj < nb - 1, zn_ref[...], 0.0) if base + sub == tb else z_ref[base + sub:base + sub + 8, :]
        neigh = _shift_time(z3, prev8, 1) + _shift_time(z3, next8, -1)
        yield
        zs = (z3 + (0.5 * neigh - z3) * mu_ref[...]).reshape(sub, RWKV_PAD)
        yield
        r = zs[:, 0:W]
        k = zs[:, W:2 * W]
        v = zs[:, 2 * W:3 * W]
        dw = zs[:, 3 * W:3 * W + 2 * LORA_RANK]
        da = zs[:, 3 * W + 2 * LORA_RANK:3 * W + 4 * LORA_RANK]
        dg = zs[:, 3 * W + 4 * LORA_RANK:3 * W + 4 * LORA_RANK + GATE_RANK]

        lw = -DECAY_SCALE * jax.nn.sigmoid(_dot(jnp.tanh(dw).astype(BF16), w2_ref[...]) + w0_ref[...])
        yield
        dab = da.astype(BF16)
        a = jax.nn.sigmoid(_dot(dab, a2_ref[...]) + a0_ref[...])
        kdir = k * (1.0 + (a - 1.0) * ka_ref[...])
        yield
        kk = k * kk_ref[...]
        kk = kk * lax.rsqrt(jnp.maximum(_seg_sum(kk * kk, ones_bd), 1e-24))
        b = kk * a
        yield

        hi, lo = _split2(lw)
        g = _dot(tri, hi) + _dot(tri, lo)
        ends = [(c if reverse else c + 1) * chunk - (0 if reverse else 1) for c in range(nchunk)]
        gl = jnp.concatenate([jnp.broadcast_to(g[e:e + 1], (chunk, W)) for e in ends], axis=0)
        yield
        e_g = jnp.exp(g)
        e_ng = jnp.exp(-g)
        e_gl = jnp.exp(gl)
        yield
        kt = kdir * e_ng
        bt = b * e_ng
        qk_s[blk, :] = (kk * jnp.exp(g - lw)).astype(BF16)
        rt_s[blk, :] = (r * e_g).astype(BF16)
        yield
        kt_s[blk, :] = kt.astype(BF16)
        bt_s[blk, :] = bt.astype(BF16)
        ktg_s[blk, :] = (kt * e_gl).astype(BF16)
        yield
        btg_s[blk, :] = (-(bt * e_gl)).astype(BF16)
        v_s[blk, :] = v.astype(BF16)
        egl_s[blk, :] = e_gl
        tokens.append(e_gl[0:8, 0:LANES])
        if final:
            yield
            a_f = jax.nn.sigmoid(_dot(dab, a2f_ref[...]) + a0f_ref[...])
            kdir_f = k * (1.0 + (a_f - 1.0) * kaf_ref[...])
            bonus = _seg_sum(r * (kdir_f + kdir) * rk_ref[...], ones_bd) * v
            yield
            gate = _dot(jax.nn.sigmoid(dg).astype(BF16), g2_ref[...])
            saved[base] = (bonus, gate)
            tokens.append(gate[0:8, 0:LANES] + bonus[0:8, 0:LANES])

    def chains(bases):
        ids = [(base // chunk + c, hf) for base in bases for c in order for hf in range(halves)]
        n = len(ids)

        def rows(c):
            return slice(c * chunk, (c + 1) * chunk)

        def slot(c, hf):
            return c * halves + hf

        a_k, b_k, b_b, pw, inv = [], [], [], [], []
        for c, hf in ids:
            sl, ls = rows(c), lanes(hf)
            x2 = jnp.concatenate([qk_s[sl, ls], rt_s[sl, ls]], axis=0)
            gk = _dot_nt(x2, stack(kt_s[sl, ls]))
            gb = _dot_nt(x2, stack(bt_s[sl, ls]))
            a_k.append(jnp.where(strict, gk[:chunk], 0.0).astype(BF16))
            b_k.append(jnp.where(incl, gk[chunk:], 0.0).astype(BF16))
            b_b.append(jnp.where(incl, gb[chunk:], 0.0).astype(BF16))
            inv.append(jnp.where(strict, -gb[:chunk], 0.0))
            yield
        for i in range(n):
            pw.append(_dot(inv[i].astype(BF16), blockdiag(inv[i])))
            inv[i] = eye_w + inv[i]
            yield
        for level in range(1, n_levels):
            for i in range(n):
                bdp = blockdiag(pw[i])
                if level < n_levels - 1:
                    both = _dot(jnp.concatenate([pw[i], inv[i]], axis=0).astype(BF16), bdp)
                    pw[i] = both[:chunk]
                    inv[i] = inv[i] + both[chunk:]
                else:
                    inv[i] = inv[i] + _dot(inv[i].astype(BF16), bdp)
                yield
        tw = [t.astype(BF16) for t in inv]
        vm, akv, qp, vp = [], [], [], []
        for i, (c, hf) in enumerate(ids):
            vm.append(stack(v_s[rows(c), lanes(hf)]))
            akv.append(_dot(a_k[i], vm[i]))
            yield
        for i, (c, hf) in enumerate(ids):
            qp.append(_dot(tw[i], stack(qk_s[rows(c), lanes(hf)])).astype(BF16))
            yield
        for i in range(n):
            vp.append(_dot(tw[i], stack(akv[i].astype(BF16))).astype(BF16))
            yield
        for i, (c, hf) in enumerate(ids):
            sl, ls = rows(c), lanes(hf)
            rp_s[sl, ls] = (rt_s[sl, ls].astype(F32) - _dot(b_b[i], stack(qp[i]))).astype(BF16)
            yield
        for i, (c, hf) in enumerate(ids):
            y_s[rows(c), lanes(hf)] = _dot(b_k[i], vm[i]) - _dot(b_b[i], stack(vp[i]))
            yield
        for i, (c, hf) in enumerate(ids):
            en = _dot_tn(qp[i], btg_s[rows(c), lanes(hf)])
            en_s[slot(c, hf)] = (en * st_mask).astype(BF16)
            yield
        for i, (c, hf) in enumerate(ids):
            sl, ls = rows(c), lanes(hf)
            fm = _dot_tn(jnp.concatenate([v_s[sl, ls], vp[i]], axis=0),
                         jnp.concatenate([ktg_s[sl, ls], btg_s[sl, ls]], axis=0))
            fm = fm * st_mask
            if i == n - 1:
                while tokens:
                    fm = fm + _zero_of(tokens.pop())[0:1, 0:1]
            f_s[slot(c, hf)] = fm
            yield

    def walk(base):
        for c in order:
            sl = slice(base + c * chunk, base + (c + 1) * chunk)
            for hf in range(halves):
                ls = lanes(hf)
                sidx = (base // chunk + c) * halves + hf
                st = s_ref[hf]
                stb = st.astype(BF16)
                y_s[sl, ls] = y_s[sl, ls] + _dot_nt(rp_s[sl, ls], stb)
                decay = egl_s[sl, ls][0:1]
                s_ref[hf] = st * decay + _dot(stb, en_s[sidx]) + f_s[sidx]
            yield

    def tail(base):
        blk = slice(base, base + sub)
        if not final:
            y = y_s[blk, :]
            o_ref[blk, :] = y
            tokens.append(y[0:8, 0:LANES])
            return
        bonus, gate = saved.pop(base)
        y = yf_ref[blk, :] + y_s[blk, :]
        inv_n = 1.0 / RWKV_HEAD
        mean = _seg_sum(y, ones_bd) * inv_n
        yield
        yc = y - mean
        var = _seg_sum(yc * yc, ones_bd) * inv_n
        yield
        yn = yc * lax.rsqrt(var + GN_EPS) * gnw_ref[...] + gnb_ref[...]
        out = (yn + bonus) * gate
        o_ref[blk, :] = out.astype(o_ref.dtype)
        tokens.append(out[sub - 8:sub, 0:LANES])

    def walk_then_tail(base):
        yield from walk(base)
        yield from tail(base)

    def drain(gen):
        for _ in gen:
            pass

    def weave(main, sides, stride):
        sides, finished = list(sides), object()
        for step, _ in enumerate(main):
            if step % stride == 0:
                sides = [g for g in sides if next(g, finished) is not finished]
        for g in sides:
            drain(g)

    nsub = tb // sub
    bases = [sb * sub for sb in (range(nsub - 1, -1, -1) if reverse else range(nsub))]
    drain(prep(bases[0]))
    tokens.clear()
    for idx, base in enumerate(bases):
        sides = []
        if idx > 0:
            sides.append(walk_then_tail(bases[idx - 1]))
        if idx + 1 < nsub:
            sides.append(prep(bases[idx + 1]))
        weave(chains([base]), sides, 1)
    drain(walk_then_tail(bases[-1]))


def _rwkv_masks(tb, chunk, reverse):
    t = np.arange(tb)
    same = (t[:, None] // chunk) == (t[None, :] // chunk)
    before = (t[None, :] >= t[:, None]) if reverse else (t[None, :] <= t[:, None])
    h = np.arange(HALF)

    def bd(row_block, col_block):
        return (h[:, None] // row_block) == (h[None, :] // col_block)

    bf = lambda m: m.astype(np.float32).astype(jnp.bfloat16)
    return [bf(same & before), bf(bd(RWKV_HEAD, RWKV_HEAD)), bf(bd(chunk, RWKV_HEAD)),
            bf(bd(chunk, chunk)), bd(RWKV_HEAD, RWKV_HEAD).astype(np.float32)]


def _rwkv(p, yf, wl, d, final):
    bsz, seq, _ = p.shape
    tb = min(TB_RWKV, seq)
    nb = seq // tb
    reverse = d == 1
    W = RWKV_WIDTH
    in_specs = _halo_specs(tb, nb, RWKV_PAD, 0, reverse)
    args = [p, p, p]
    blk = lambda b, j: (b, _time_block(j, nb, reverse), 0)
    const = lambda shape: pl.BlockSpec(shape, lambda b, j: (0,) * len(shape))
    if final:
        in_specs += [pl.BlockSpec((None, tb, W), blk)]
        args += [yf]
    sub = min(SUB_RWKV, tb)
    masks = _rwkv_masks(sub, CHUNK, reverse)
    in_specs += [const(m.shape) for m in masks]
    args += [jnp.asarray(m) for m in masks]
    in_specs += [const((1, RWKV_PAD)), const((1, W)), const((2 * LORA_RANK, W)), const((1, W)),
                 const((2 * LORA_RANK, W)), const((1, W)), const((1, W))]
    args += [wl["rwkv_mu"], wl["rwkv_w0"][d], wl["rwkv_w2"][d], wl["rwkv_a0"][d], wl["rwkv_a2"][d],
             wl["rwkv_k_k"], wl["rwkv_k_a"][d]]
    if final:
        in_specs += [const((1, W)), const((2 * LORA_RANK, W)), const((1, W)), const((GATE_RANK, W)),
                     const((1, W)), const((1, W)), const((1, W))]
        args += [wl["rwkv_a0"][0], wl["rwkv_a2"][0], wl["rwkv_k_a"][0], wl["rwkv_g2"], wl["rwkv_r_k"],
                 wl["rwkv_gn_w"], wl["rwkv_gn_b"]]
    scratch = [pltpu.VMEM((W // HALF, HALF, HALF), F32)]
    scratch += [pltpu.VMEM((tb, W), BF16)] * 8
    scratch += [pltpu.VMEM((tb, W), F32)] * 2
    nhc = (tb // CHUNK) * (W // HALF)
    scratch += [pltpu.VMEM((nhc, HALF, HALF), BF16), pltpu.VMEM((nhc, HALF, HALF), F32)]
    return pl.pallas_call(
        functools.partial(_rwkv_kernel, reverse=reverse, final=final, tb=tb, nb=nb, sub=sub, chunk=CHUNK),
        grid=(bsz, nb),
        in_specs=in_specs,
        out_specs=pl.BlockSpec((None, tb, W), blk),
        out_shape=jax.ShapeDtypeStruct((bsz, seq, W), BF16 if final else F32),
        scratch_shapes=scratch,
        compiler_params=_params(("arbitrary", "arbitrary")),
        name="rwkv_bwd" if final else "rwkv_fwd",
    )(*args)


def _t5_bucket_table():
    qi = np.arange(ATTN_BLOCK)[:, None]
    kj = np.arange(3 * ATTN_BLOCK)[None, :]
    rel = kj - ATTN_BLOCK - qi
    half = N_BUCKETS // 2
    max_exact = half // 2
    ret = np.where(rel > 0, half, 0)
    n = np.abs(rel)
    nf = np.maximum(n, 1).astype(np.float32)
    large = max_exact + (np.log(nf / max_exact) / math.log(MAX_DISTANCE / max_exact)
                         * (half - max_exact)).astype(np.int32)
    large = np.minimum(large, half - 1)
    return (ret + np.where(n < max_exact, n, large)).astype(np.int32)


def _attn_kernel(q_ref, kp_ref, kc_ref, kn_ref, vp_ref, vc_ref, vn_ref, bkt_ref, rb_ref, sink_ref,
                 o_ref, bias_ref, *, nb):
    b = pl.program_id(0)
    j = pl.program_id(1)
    blk = ATTN_BLOCK

    @pl.when((b == 0) & (j == 0))
    def _():
        bucket = bkt_ref[...]
        qi = lax.broadcasted_iota(jnp.int32, (blk, 3 * blk), 0)
        kj = lax.broadcasted_iota(jnp.int32, (blk, 3 * blk), 1)
        in_window = jnp.abs(kj - blk - qi) <= WINDOW
        for h in range(N_Q_HEADS):
            def body(t, acc, h=h):
                return jnp.where(bucket == t, rb_ref[t, h], acc)
            bias = lax.fori_loop(0, N_BUCKETS, body, jnp.zeros((blk, 3 * blk), F32))
            bias = jnp.where(in_window, bias, NEG)
            bias_ref[0, h] = bias
            bias_ref[1, h] = jnp.where(kj < blk, NEG, bias)
            bias_ref[2, h] = jnp.where(kj >= 2 * blk, NEG, bias)

    variant = jnp.where(j == 0, 1, jnp.where(j == nb - 1, 2, 0))
    hrow = lax.broadcasted_iota(jnp.int32, (GQA_GROUP * blk, 1), 0) // blk
    kcat = jnp.concatenate([kp_ref[...], kc_ref[...], kn_ref[...]], axis=0).astype(BF16)
    vcat = jnp.concatenate([vp_ref[...], vc_ref[...], vn_ref[...]], axis=0).astype(BF16)
    for g in range(N_KV_HEADS):
        qs = jnp.concatenate(
            [q_ref[:, HEAD_DIM * (GQA_GROUP * g + hh):HEAD_DIM * (GQA_GROUP * g + hh + 1)] for hh in range(GQA_GROUP)],
            axis=0).astype(BF16)
        kg = kcat[:, HEAD_DIM * g:HEAD_DIM * (g + 1)]
        vg = vcat[:, HEAD_DIM * g:HEAD_DIM * (g + 1)]
        bias = bias_ref[variant, GQA_GROUP * g:GQA_GROUP * (g + 1)].reshape(GQA_GROUP * blk, 3 * blk)
        s = _dot_nt(qs, kg) + bias
        sk = jnp.zeros((GQA_GROUP * blk, 1), F32)
        for hh in range(GQA_GROUP):
            sk = jnp.where(hrow == hh, sink_ref[0, GQA_GROUP * g + hh], sk)
        m = jnp.maximum(jnp.max(s, axis=-1, keepdims=True), sk)
        pexp = jnp.exp(s - m)
        den = jnp.sum(pexp, axis=-1, keepdims=True) + jnp.exp(sk - m)
        o = _dot(pexp.astype(BF16), vg) / den
        for hh in range(GQA_GROUP):
            h = GQA_GROUP * g + hh
            o_ref[:, HEAD_DIM * h:HEAD_DIM * (h + 1)] = o[blk * hh:blk * (hh + 1)].astype(o_ref.dtype)


def _attn(p, wl):
    bsz, seq, _ = p.shape
    blk = ATTN_BLOCK
    nb = seq // blk
    assert nb >= 2, "the first and the last block of a sequence use different bias variants"
    kcol, vcol = P_K // KV_WIDTH, P_V // KV_WIDTH

    def kv(col, off):
        return pl.BlockSpec((None, blk, KV_WIDTH), lambda b, j: (b, jnp.clip(j + off, 0, nb - 1), col))

    smem = lambda: pl.BlockSpec(memory_space=pltpu.SMEM)
    return pl.pallas_call(
        functools.partial(_attn_kernel, nb=nb),
        grid=(bsz, nb),
        in_specs=[pl.BlockSpec((None, blk, ATTN_WIDTH), lambda b, j: (b, j, P_Q // ATTN_WIDTH)),
                  kv(kcol, -1), kv(kcol, 0), kv(kcol, 1), kv(vcol, -1), kv(vcol, 0), kv(vcol, 1),
                  pl.BlockSpec((blk, 3 * blk), lambda b, j: (0, 0)), smem(), smem()],
        out_specs=pl.BlockSpec((None, blk, ATTN_WIDTH), lambda b, j: (b, j, 0)),
        out_shape=jax.ShapeDtypeStruct((bsz, seq, ATTN_WIDTH), BF16),
        scratch_shapes=[pltpu.VMEM((3, N_Q_HEADS, blk, 3 * blk), F32)],
        compiler_params=_params(("arbitrary", "arbitrary")),
        name="attn",
    )(p, p, p, p, p, p, p, wl["bucket"], wl["rel_bias"], wl["attn_sink"])


def _outproj_kernel(lru_ref, rwkv_ref, attn_ref, x_ref, w_ref, g_ref, o_ref):
    mix = (_dot(lru_ref[...], w_ref[0:LRU_WIDTH, :])
           + _dot(rwkv_ref[...], w_ref[LRU_WIDTH:LRU_WIDTH + RWKV_WIDTH, :])
           + _dot(attn_ref[...], w_ref[LRU_WIDTH + RWKV_WIDTH:, :]))
    ms = jnp.mean(mix * mix, axis=-1, keepdims=True)
    o_ref[...] = x_ref[...] + mix * lax.rsqrt(ms + EPS) * g_ref[...]


def _outproj(lru, rwkv, attn, x2, w, gain):
    t = x2.shape[0]
    tm = min(TM_OUT, t)
    row = lambda width: pl.BlockSpec((tm, width), lambda i: (i, 0))
    return pl.pallas_call(
        _outproj_kernel,
        grid=(t // tm,),
        in_specs=[row(LRU_WIDTH), row(RWKV_WIDTH), row(ATTN_WIDTH), row(D_MODEL),
                  pl.BlockSpec((D_MODEL, D_MODEL), lambda i: (0, 0)),
                  pl.BlockSpec((1, D_MODEL), lambda i: (0, 0))],
        out_specs=row(D_MODEL),
        out_shape=jax.ShapeDtypeStruct((t, D_MODEL), F32),
        compiler_params=_params(("arbitrary",)),
        name="outproj",
    )(lru, rwkv, attn, x2, w, gain)


def _ffn_kernel(x_ref, gpre_ref, wu_ref, wd_ref, gpost_ref, o_ref, h_ref, acc_ref):
    f = pl.program_id(1)

    @pl.when(f == 0)
    def _():
        x = x_ref[...]
        ms = jnp.mean(x * x, axis=-1, keepdims=True)
        h_ref[...] = (x * lax.rsqrt(ms + EPS) * gpre_ref[...]).astype(BF16)
        acc_ref[...] = jnp.zeros_like(acc_ref)

    u = jnp.maximum(_dot(h_ref[...], wu_ref[...]), 0.0)
    acc_ref[...] += _dot((u * u).astype(BF16), wd_ref[...])

    @pl.when(f == pl.num_programs(1) - 1)
    def _():
        y = acc_ref[...]
        ms = jnp.mean(y * y, axis=-1, keepdims=True)
        o_ref[...] = x_ref[...] + y * lax.rsqrt(ms + EPS) * gpost_ref[...]


def _ffn(x2, gpre, wu, wd, gpost):
    t = x2.shape[0]
    tm = min(TM_FFN, t)
    return pl.pallas_call(
        _ffn_kernel,
        grid=(t // tm, D_FF // TF_FFN),
        in_specs=[pl.BlockSpec((tm, D_MODEL), lambda i, f: (i, 0)),
                  pl.BlockSpec((1, D_MODEL), lambda i, f: (0, 0)),
                  pl.BlockSpec((D_MODEL, TF_FFN), lambda i, f: (0, f)),
                  pl.BlockSpec((TF_FFN, D_MODEL), lambda i, f: (f, 0)),
                  pl.BlockSpec((1, D_MODEL), lambda i, f: (0, 0))],
        out_specs=pl.BlockSpec((tm, D_MODEL), lambda i, f: (i, 0)),
        out_shape=jax.ShapeDtypeStruct((t, D_MODEL), F32),
        scratch_shapes=[pltpu.VMEM((tm, D_MODEL), BF16), pltpu.VMEM((tm, D_MODEL), F32)],
        compiler_params=_params(("arbitrary", "arbitrary")),
        name="ffn",
    )(x2, gpre, wu, wd, gpost)


def _block_diag(w):
    n, bi, bj = w.shape
    eye = jnp.eye(n, dtype=w.dtype)
    return (eye[:, None, :, None] * w[:, :, None, :]).reshape(n * bi, n * bj)


def _pad_lora(w2, d):
    z = jnp.zeros_like(w2[d])
    return jnp.concatenate([w2[0], z] if d == 0 else [z, w2[1]], axis=0)


def _layer_weights(l, w):
    row = lambda v: v.reshape(1, -1).astype(F32)
    w_in = w["w_in"][l]
    w_in_p = jnp.concatenate(
        [w_in[:, OFF_RWKV:OFF_ATTN], jnp.zeros((D_MODEL, RWKV_PAD - RWKV_COLS), w_in.dtype),
         w_in[:, 0:OFF_RWKV], w_in[:, OFF_ATTN:OFF_ATTN + ATTN_WIDTH] * (HEAD_DIM ** -0.5),
         w_in[:, OFF_ATTN + ATTN_WIDTH:]], axis=1).astype(BF16)
    mu = jnp.concatenate([w["rwkv_mu"][l], jnp.zeros((RWKV_PAD - RWKV_COLS,), F32)]).reshape(1, -1)
    return {
        "norm_mix_pre": row(w["norm_mix_pre"][l]), "norm_mix_post": row(w["norm_mix_post"][l]),
        "norm_ffn_pre": row(w["norm_ffn_pre"][l]), "norm_ffn_post": row(w["norm_ffn_post"][l]),
        "w_in": w_in_p, "w_out": w["w_out"][l].astype(BF16),
        "conv_w": w["conv_w"][l].astype(F32), "conv_b": row(w["conv_b"][l]),
        "lru_wa": [_block_diag(w["lru_wa"][l, d]).astype(BF16) for d in range(2)],
        "lru_wx": [_block_diag(w["lru_wx"][l, d]).astype(BF16) for d in range(2)],
        "lru_ba": [row(w["lru_ba"][l, d]) for d in range(2)],
        "lru_bx": [row(w["lru_bx"][l, d]) for d in range(2)],
        "lru_lambda": [row(w["lru_lambda"][l, d]) for d in range(2)],
        "rwkv_mu": mu,
        "rwkv_w0": [row(w["rwkv_w0"][l, d]) for d in range(2)],
        "rwkv_w2": [_pad_lora(w["rwkv_w2"][l], d).astype(BF16) for d in range(2)],
        "rwkv_a0": [row(w["rwkv_a0"][l, d]) for d in range(2)],
        "rwkv_a2": [_pad_lora(w["rwkv_a2"][l], d).astype(BF16) for d in range(2)],
        "rwkv_g2": w["rwkv_g2"][l].astype(BF16),
        "rwkv_k_k": row(w["rwkv_k_k"][l]),
        "rwkv_k_a": [row(w["rwkv_k_a"][l, d]) for d in range(2)],
        "rwkv_r_k": row(w["rwkv_r_k"][l]),
        "rwkv_gn_w": row(w["rwkv_gn_w"][l]), "rwkv_gn_b": row(w["rwkv_gn_b"][l]),
        "attn_sink": w["attn_sink"][l].reshape(1, -1).astype(F32),
        "rel_bias": w["rel_bias"].astype(F32),
        "bucket": jnp.asarray(_t5_bucket_table()),
        "w_up": w["w_up"][l].astype(BF16), "w_down": w["w_down"][l].astype(BF16),
    }


def _trunk(x, layers):
    bsz, seq, _ = x.shape
    t = bsz * seq
    x2 = x.reshape(t, D_MODEL)
    for wl in layers:
        p = _inproj(x2, wl["norm_mix_pre"], wl["w_in"]).reshape(bsz, seq, P_COLS)
        hf = _lru(p, None, wl, 0, False)
        lru_out = _lru(p, hf, wl, 1, True)
        yf = _rwkv(p, None, wl, 0, False)
        rwkv_out = _rwkv(p, yf, wl, 1, True)
        attn_out = _attn(p, wl)
        x2 = _outproj(lru_out.reshape(t, LRU_WIDTH), rwkv_out.reshape(t, RWKV_WIDTH),
                      attn_out.reshape(t, ATTN_WIDTH), x2, wl["w_out"], wl["norm_mix_post"])
        x2 = _ffn(x2, wl["norm_ffn_pre"], wl["w_up"], wl["w_down"], wl["norm_ffn_post"])
    return x2.reshape(bsz, seq, D_MODEL)


def kernel(x_prompt, x_sample, norm_mix_pre, norm_mix_post, norm_ffn_pre, norm_ffn_post, w_in, w_out, conv_w, conv_b, lru_wa, lru_ba, lru_wx, lru_bx, lru_lambda, rwkv_mu, rwkv_w0, rwkv_w2, rwkv_a0, rwkv_a2, rwkv_g2, rwkv_k_k, rwkv_k_a, rwkv_r_k, rwkv_gn_w, rwkv_gn_b, attn_sink, rel_bias, w_up, w_down):
    w = dict(norm_mix_pre=norm_mix_pre, norm_mix_post=norm_mix_post, norm_ffn_pre=norm_ffn_pre,
             norm_ffn_post=norm_ffn_post, w_in=w_in, w_out=w_out, conv_w=conv_w, conv_b=conv_b,
             lru_wa=lru_wa, lru_ba=lru_ba, lru_wx=lru_wx, lru_bx=lru_bx, lru_lambda=lru_lambda,
             rwkv_mu=rwkv_mu, rwkv_w0=rwkv_w0, rwkv_w2=rwkv_w2, rwkv_a0=rwkv_a0, rwkv_a2=rwkv_a2,
             rwkv_g2=rwkv_g2, rwkv_k_k=rwkv_k_k, rwkv_k_a=rwkv_k_a, rwkv_r_k=rwkv_r_k,
             rwkv_gn_w=rwkv_gn_w, rwkv_gn_b=rwkv_gn_b, attn_sink=attn_sink, rel_bias=rel_bias,
             w_up=w_up, w_down=w_down)
    layers = [_layer_weights(l, w) for l in range(w_in.shape[0])]
    return (_trunk(x_prompt, layers), _trunk(x_sample, layers))
```

```python
import functools
import math

import numpy as np
import jax
import jax.numpy as jnp
from jax import lax
from jax.experimental import pallas as pl
from jax.experimental.pallas import tpu as pltpu

F32 = jnp.float32
BF16 = jnp.bfloat16

D_MODEL = 2048
LRU_WIDTH = 512
LRU_BLOCKS = 8
LRU_C = 8.0
RWKV_WIDTH = 512
RWKV_HEAD = 64
RWKV_HEADS = 8
LORA_RANK = 64
GATE_RANK = 128
DECAY_SCALE = 0.606531
GN_EPS = 64e-5
ATTN_WIDTH = 1024
HEAD_DIM = 128
N_Q_HEADS = 8
N_KV_HEADS = 2
GQA_GROUP = 4
KV_WIDTH = 256
WINDOW = 128
ATTN_BLOCK = 128
N_BUCKETS = 32
MAX_DISTANCE = 128
D_FF = 8192
EPS = 1e-6
OFF_RWKV = 2 * LRU_WIDTH
RWKV_COLS = 3 * RWKV_WIDTH + 4 * LORA_RANK + GATE_RANK
OFF_ATTN = OFF_RWKV + RWKV_COLS

RWKV_PAD = 2048
P_LRU_X = 2048
P_LRU_G = 2560
P_Q = 3072
P_K = 4096
P_V = 4352
P_COLS = 4608

LANES = 128
HALF = 256
VMEM_LIMIT = 56 * 1024 * 1024

TM_PROJ = 1024
TN_PROJ = 1536
TM_OUT = 512
TM_FFN = 512
TF_FFN = 1024
TB_LRU = 256
TB_RWKV = 512
SUB_RWKV = 256
CHUNK = 64
NEG = -1e30


def _dot(a, b):
    return jnp.dot(a, b, preferred_element_type=F32)


def _dot_nt(a, b):
    return lax.dot_general(a, b, (((1,), (1,)), ((), ())), preferred_element_type=F32)


def _dot_tn(a, b):
    return lax.dot_general(a, b, (((0,), (0,)), ((), ())), preferred_element_type=F32)


def _split2(x):
    hi = x.astype(BF16)
    lo = (x - hi.astype(F32)).astype(BF16)
    return hi, lo


def _params(sem):
    return pltpu.CompilerParams(dimension_semantics=sem, vmem_limit_bytes=VMEM_LIMIT)


def _inproj_kernel(x_ref, g_ref, w_ref, o_ref, h_ref):
    @pl.when(pl.program_id(1) == 0)
    def _():
        x = x_ref[...]
        ms = jnp.mean(x * x, axis=-1, keepdims=True)
        h_ref[...] = (x * lax.rsqrt(ms + EPS) * g_ref[...]).astype(BF16)

    o_ref[...] = _dot(h_ref[...], w_ref[...])


def _inproj(x2, gain, w):
    t = x2.shape[0]
    tm = min(TM_PROJ, t)
    return pl.pallas_call(
        _inproj_kernel,
        grid=(t // tm, P_COLS // TN_PROJ),
        in_specs=[
            pl.BlockSpec((tm, D_MODEL), lambda i, j: (i, 0)),
            pl.BlockSpec((1, D_MODEL), lambda i, j: (0, 0)),
            pl.BlockSpec((D_MODEL, TN_PROJ), lambda i, j: (0, j)),
        ],
        out_specs=pl.BlockSpec((tm, TN_PROJ), lambda i, j: (i, j)),
        out_shape=jax.ShapeDtypeStruct((t, P_COLS), F32),
        scratch_shapes=[pltpu.VMEM((tm, D_MODEL), BF16)],
        compiler_params=_params(("arbitrary", "arbitrary")),
        name="inproj",
    )(x2, gain, w)


def _time_block(j, nb, reverse):
    return nb - 1 - j if reverse else j


def _halo_specs(tb, nb, width, col, reverse):
    rows8 = tb // 8
    last8 = nb * rows8 - 1

    def blk(b, j):
        return (b, _time_block(j, nb, reverse), col)

    def prev(b, j):
        return (b, jnp.maximum(_time_block(j, nb, reverse) * rows8 - 1, 0), col)

    def nxt(b, j):
        return (b, jnp.minimum((_time_block(j, nb, reverse) + 1) * rows8, last8), col)

    return [pl.BlockSpec((None, tb, width), blk),
            pl.BlockSpec((None, 8, width), prev),
            pl.BlockSpec((None, 8, width), nxt)]


def _shift_time(x3, halo, back):
    r8 = lax.broadcasted_iota(jnp.int32, (1, 8, 1), 1)
    shift = back % 8
    w = pltpu.roll(x3, shift, 1)
    wh = pltpu.roll(halo, shift, 0)[None]
    if back > 0:
        return jnp.where(r8 < back, jnp.concatenate([wh, w[:-1]], axis=0), w)
    return jnp.where(r8 >= 8 + back, jnp.concatenate([w[1:], wh], axis=0), w)


def _softplus(z):
    return jnp.maximum(z, 0.0) + jnp.log(1.0 + jnp.exp(-jnp.abs(z)))


def _lru_kernel(*refs, reverse, final, tb, nb, seq):
    if final:
        xc_ref, g_ref, hf_ref, wa_ref, wx_ref, ba_ref, bx_ref, lam_ref, o_ref, carry_ref = refs
    else:
        (x_ref, xp_ref, xn_ref, cw_ref, cb_ref, wa_ref, wx_ref, ba_ref, bx_ref, lam_ref,
         o_ref, xc_ref, carry_ref) = refs
    j = pl.program_id(1)
    jj = _time_block(j, nb, reverse)

    @pl.when(j == 0)
    def _():
        carry_ref[...] = jnp.zeros_like(carry_ref)

    ngroups = tb // 8
    width = xc_ref.shape[1]
    row = lax.broadcasted_iota(jnp.int32, (tb, 1), 0)
    if final:
        xc = xc_ref[...]
    else:
        x3 = x_ref[...].reshape(ngroups, 8, width)
        prev8 = jnp.where(jj > 0, xp_ref[...], 0.0)
        next8 = jnp.where(jj < nb - 1, xn_ref[...], 0.0)
        cw = cw_ref[...]
        xc = (cw[0:1] * _shift_time(x3, prev8, 2) + cw[1:2] * _shift_time(x3, prev8, 1) + cw[2:3] * x3
              + cw[3:4] * _shift_time(x3, next8, -1) + cb_ref[...]).reshape(tb, width)
        xc_ref[...] = xc

    xcb = xc.astype(BF16)
    r = jax.nn.sigmoid(_dot(xcb, wa_ref[...]) + ba_ref[...])
    i = jax.nn.sigmoid(_dot(xcb, wx_ref[...]) + bx_ref[...])
    a = jnp.exp2(r * ((-LRU_C * math.log2(math.e)) * _softplus(-lam_ref[...])))
    om = 1.0 - a * a
    mult = jnp.where(om > 0.0, om * lax.rsqrt(om), 0.0)
    gpos = jj * tb + row
    first = (gpos == seq - 1) if reverse else (gpos == 0)
    mult = jnp.where(first, 1.0, mult)
    u = mult * i * xc

    r8 = lax.broadcasted_iota(jnp.int32, (1, 8, 1), 1)
    acc_a, acc_u = a.reshape(ngroups, 8, width), u.reshape(ngroups, 8, width)
    for s in (1, 2, 4):
        shift, valid = (8 - s, r8 < 8 - s) if reverse else (s, r8 >= s)
        sh_a, sh_u = pltpu.roll(acc_a, shift, 1), pltpu.roll(acc_u, shift, 1)
        acc_u = acc_a * jnp.where(valid, sh_u, 0.0) + acc_u
        acc_a = acc_a * jnp.where(valid, sh_a, 1.0)
    acc_a, acc_u = acc_a.reshape(tb, width), acc_u.reshape(tb, width)
    last_row = 0 if reverse else 7
    carry = carry_ref[...]
    carries = [None] * ngroups
    for gi in (range(ngroups - 1, -1, -1) if reverse else range(ngroups)):
        carries[gi] = carry
        at = 8 * gi + last_row
        carry = (jnp.broadcast_to(acc_a[at:at + 1], carry.shape) * carry
                 + jnp.broadcast_to(acc_u[at:at + 1], carry.shape))
    carry_ref[...] = carry
    h = acc_a * jnp.concatenate(carries, axis=0) + acc_u

    if final:
        o_ref[...] = ((hf_ref[...] + h) * jax.nn.gelu(g_ref[...], approximate=True)).astype(o_ref.dtype)
    else:
        o_ref[...] = h


def _lru(p, fwd, wl, d, final):
    bsz, seq, _ = p.shape
    tb = min(TB_LRU, seq)
    nb = seq // tb
    reverse = d == 1
    blk = lambda b, j: (b, _time_block(j, nb, reverse), 0)
    rows = pl.BlockSpec((None, tb, LRU_WIDTH), blk)
    const = lambda shape: pl.BlockSpec(shape, lambda b, j: (0,) * len(shape))
    if final:
        hf, xc = fwd
        in_specs = [rows, pl.BlockSpec((None, tb, LRU_WIDTH),
                                       lambda b, j: (b, _time_block(j, nb, reverse), P_LRU_G // LRU_WIDTH)), rows]
        args = [xc, p, hf]
        out_specs = rows
        out_shape = jax.ShapeDtypeStruct((bsz, seq, LRU_WIDTH), BF16)
    else:
        in_specs = _halo_specs(tb, nb, LRU_WIDTH, P_LRU_X // LRU_WIDTH, reverse)
        in_specs += [const((4, LRU_WIDTH)), const((1, LRU_WIDTH))]
        args = [p, p, p, wl["conv_w"], wl["conv_b"]]
        out_specs = [rows, rows]
        out_shape = [jax.ShapeDtypeStruct((bsz, seq, LRU_WIDTH), F32)] * 2
    in_specs += [const((LRU_WIDTH, LRU_WIDTH)), const((LRU_WIDTH, LRU_WIDTH)), const((1, LRU_WIDTH)),
                 const((1, LRU_WIDTH)), const((1, LRU_WIDTH))]
    args += [wl["lru_wa"][d], wl["lru_wx"][d], wl["lru_ba"][d], wl["lru_bx"][d], wl["lru_lambda"][d]]
    return pl.pallas_call(
        functools.partial(_lru_kernel, reverse=reverse, final=final, tb=tb, nb=nb, seq=seq),
        grid=(bsz, nb),
        in_specs=in_specs,
        out_specs=out_specs,
        out_shape=out_shape,
        scratch_shapes=[pltpu.VMEM((8, LRU_WIDTH), F32)],
        compiler_params=_params(("arbitrary", "arbitrary")),
        name="lru_bwd" if final else "lru_fwd",
    )(*args)


def _seg_sum(x, ones_bd):
    xb = x.astype(BF16)
    return jnp.concatenate(
        [_dot(xb[:, HALF * hf:HALF * (hf + 1)], ones_bd) for hf in range(x.shape[1] // HALF)], axis=1)


def _zero_of(x):
    bits = pltpu.bitcast(x, jnp.uint32)
    return ((bits >> 16) >> 16).astype(F32)


def _rwkv_kernel(*refs, reverse, final, tb, nb, sub, chunk):
    if final:
        (zs_ref, kkn_ref, yf_ref, tri_ref, ones_ref, hm_ref, bdm_ref, stm_ref,
         w0_ref, w2_ref, a0_ref, a2_ref, ka_ref,
         a0f_ref, a2f_ref, kaf_ref, g2_ref, rk_ref, gnw_ref, gnb_ref,
         o_ref, s_ref, qk_s, rt_s, kt_s, bt_s, ktg_s, btg_s, v_s, rp_s, egl_s, y_s, en_s, f_s) = refs
    else:
        (z_ref, zp_ref, zn_ref, tri_ref, ones_ref, hm_ref, bdm_ref, stm_ref,
         mu_ref, w0_ref, w2_ref, a0_ref, a2_ref, kk_ref, ka_ref,
         o_ref, zs_ref, kkn_ref, s_ref, qk_s, rt_s, kt_s, bt_s, ktg_s, btg_s, v_s, rp_s, egl_s, y_s, en_s,
         f_s) = refs
    j = pl.program_id(1)
    jj = _time_block(j, nb, reverse)
    W = RWKV_WIDTH
    group = HALF // chunk
    halves = W // HALF
    nchunk = sub // chunk
    n_levels = int(math.log2(chunk))
    order = range(nchunk - 1, -1, -1) if reverse else range(nchunk)

    @pl.when(j == 0)
    def _():
        s_ref[...] = jnp.zeros_like(s_ref)

    ones_bd = ones_ref[...]
    tri = tri_ref[...]
    head_mask = hm_ref[...]
    bd_mask = bdm_ref[...]
    st_mask = stm_ref[...]
    wi = lax.broadcasted_iota(jnp.int32, (chunk, HALF), 0)
    wj = lax.broadcasted_iota(jnp.int32, (chunk, HALF), 1) % chunk
    strict = (wj > wi) if reverse else (wj < wi)
    incl = (wj >= wi) if reverse else (wj <= wi)
    eye_w = jnp.where(wi == wj, 1.0, 0.0)

    def stack(xb):
        return jnp.concatenate([xb] * group, axis=0) * head_mask

    def blockdiag(xw):
        return jnp.concatenate([xw.astype(BF16)] * group, axis=0) * bd_mask

    def lanes(hf):
        return slice(HALF * hf, HALF * hf + HALF)

    saved = {}
    tokens = []

    def prep(base):
        blk = slice(base, base + sub)
        if final:
            zs = zs_ref[blk, :]
        else:
            z3 = z_ref[blk, :].reshape(sub // 8, 8, RWKV_PAD)
            prev8 = jnp.where(jj > 0, zp_ref[...], 0.0) if base == 0 else z_ref[base - 8:base, :]
            next8 = (jnp.where(jj < nb - 1, zn_ref[...], 0.0) if base + sub == tb
                     else z_ref[base + sub:base + sub + 8, :])
            neigh = _shift_time(z3, prev8, 1) + _shift_time(z3, next8, -1)
            yield
            zs = (z3 + (0.5 * neigh - z3) * mu_ref[...]).reshape(sub, RWKV_PAD)
            zs_ref[blk, :] = zs
        yield
        r = zs[:, 0:W]
        k = zs[:, W:2 * W]
        v = zs[:, 2 * W:3 * W]
        dw = zs[:, 3 * W:3 * W + 2 * LORA_RANK]
        da = zs[:, 3 * W + 2 * LORA_RANK:3 * W + 4 * LORA_RANK]
        dg = zs[:, 3 * W + 4 * LORA_RANK:3 * W + 4 * LORA_RANK + GATE_RANK]

        lw = -DECAY_SCALE * jax.nn.sigmoid(_dot(jnp.tanh(dw).astype(BF16), w2_ref[...]) + w0_ref[...])
        yield
        dab = da.astype(BF16)
        a = jax.nn.sigmoid(_dot(dab, a2_ref[...]) + a0_ref[...])
        kdir = k * (1.0 + (a - 1.0) * ka_ref[...])
        yield
        if final:
            kk = kkn_ref[blk, :]
        else:
            kk = k * kk_ref[...]
            kk = kk * lax.rsqrt(jnp.maximum(_seg_sum(kk * kk, ones_bd), 1e-24))
            kkn_ref[blk, :] = kk
        b = kk * a
        yield

        hi, lo = _split2(lw)
        g = _dot(tri, hi) + _dot(tri, lo)
        ends = [(c if reverse else c + 1) * chunk - (0 if reverse else 1) for c in range(nchunk)]
        gl = jnp.concatenate([jnp.broadcast_to(g[e:e + 1], (chunk, W)) for e in ends], axis=0)
        yield
        e_g = jnp.exp(g)
        e_ng = jnp.exp(-g)
        e_gl = jnp.exp(gl)
        yield
        kt = kdir * e_ng
        bt = b * e_ng
        qk_s[blk, :] = (kk * jnp.exp(g - lw)).astype(BF16)
        rt_s[blk, :] = (r * e_g).astype(BF16)
        yield
        kt_s[blk, :] = kt.astype(BF16)
        bt_s[blk, :] = bt.astype(BF16)
        ktg_s[blk, :] = (kt * e_gl).astype(BF16)
        yield
        btg_s[blk, :] = (-(bt * e_gl)).astype(BF16)
        v_s[blk, :] = v.astype(BF16)
        egl_s[blk, :] = e_gl
        tokens.append(e_gl[0:8, 0:LANES])
        if final:
            yield
            a_f = jax.nn.sigmoid(_dot(dab, a2f_ref[...]) + a0f_ref[...])
            kdir_f = k * (1.0 + (a_f - 1.0) * kaf_ref[...])
            bonus = _seg_sum(r * (kdir_f + kdir) * rk_ref[...], ones_bd) * v
            yield
            gate = _dot(jax.nn.sigmoid(dg).astype(BF16), g2_ref[...])
            saved[base] = (bonus, gate)
            tokens.append(gate[0:8, 0:LANES] + bonus[0:8, 0:LANES])

    def chains(bases):
        ids = [(base // chunk + c, hf) for base in bases for c in order for hf in range(halves)]
        n = len(ids)

        def rows(c):
            return slice(c * chunk, (c + 1) * chunk)

        def slot(c, hf):
            return c * halves + hf

        a_k, b_k, b_b, pw, inv = [], [], [], [], []
        for c, hf in ids:
            sl, ls = rows(c), lanes(hf)
            x2 = jnp.concatenate([qk_s[sl, ls], rt_s[sl, ls]], axis=0)
            gk = _dot_nt(x2, stack(kt_s[sl, ls]))
            gb = _dot_nt(x2, stack(bt_s[sl, ls]))
            a_k.append(jnp.where(strict, gk[:chunk], 0.0).astype(BF16))
            b_k.append(jnp.where(incl, gk[chunk:], 0.0).astype(BF16))
            b_b.append(jnp.where(incl, gb[chunk:], 0.0).astype(BF16))
            inv.append(jnp.where(strict, -gb[:chunk], 0.0))
            yield
        for i in range(n):
            pw.append(_dot(inv[i].astype(BF16), blockdiag(inv[i])))
            inv[i] = eye_w + inv[i]
            yield
        for level in range(1, n_levels):
            for i in range(n):
                bdp = blockdiag(pw[i])
                if level < n_levels - 1:
                    both = _dot(jnp.concatenate([pw[i], inv[i]], axis=0).astype(BF16), bdp)
                    pw[i] = both[:chunk]
                    inv[i] = inv[i] + both[chunk:]
                else:
                    inv[i] = inv[i] + _dot(inv[i].astype(BF16), bdp)
                yield
        tw = [t.astype(BF16) for t in inv]
        vm, akv, qp, vp = [], [], [], []
        for i, (c, hf) in enumerate(ids):
            vm.append(stack(v_s[rows(c), lanes(hf)]))
            akv.append(_dot(a_k[i], vm[i]))
            yield
        for i, (c, hf) in enumerate(ids):
            qp.append(_dot(tw[i], stack(qk_s[rows(c), lanes(hf)])).astype(BF16))
            yield
        for i in range(n):
            vp.append(_dot(tw[i], stack(akv[i].astype(BF16))).astype(BF16))
            yield
        for i, (c, hf) in enumerate(ids):
            sl, ls = rows(c), lanes(hf)
            rp_s[sl, ls] = (rt_s[sl, ls].astype(F32) - _dot(b_b[i], stack(qp[i]))).astype(BF16)
            yield
        for i, (c, hf) in enumerate(ids):
            y_s[rows(c), lanes(hf)] = _dot(b_k[i], vm[i]) - _dot(b_b[i], stack(vp[i]))
            yield
        for i, (c, hf) in enumerate(ids):
            en = _dot_tn(qp[i], btg_s[rows(c), lanes(hf)])
            en_s[slot(c, hf)] = (en * st_mask).astype(BF16)
            yield
        for i, (c, hf) in enumerate(ids):
            sl, ls = rows(c), lanes(hf)
            fm = _dot_tn(jnp.concatenate([v_s[sl, ls], vp[i]], axis=0),
                         jnp.concatenate([ktg_s[sl, ls], btg_s[sl, ls]], axis=0))
            fm = fm * st_mask
            if i == n - 1:
                while tokens:
                    fm = fm + _zero_of(tokens.pop())[0:1, 0:1]
            f_s[slot(c, hf)] = fm
            yield

    def walk(base):
        for c in order:
            sl = slice(base + c * chunk, base + (c + 1) * chunk)
            for hf in range(halves):
                ls = lanes(hf)
                sidx = (base // chunk + c) * halves + hf
                st = s_ref[hf]
                stb = st.astype(BF16)
                y_s[sl, ls] = y_s[sl, ls] + _dot_nt(rp_s[sl, ls], stb)
                decay = egl_s[sl, ls][0:1]
                s_ref[hf] = st * decay + _dot(stb, en_s[sidx]) + f_s[sidx]
            yield

    def tail(base):
        blk = slice(base, base + sub)
        if not final:
            y = y_s[blk, :]
            o_ref[blk, :] = y
            tokens.append(y[0:8, 0:LANES])
            return
        bonus, gate = saved.pop(base)
        y = yf_ref[blk, :] + y_s[blk, :]
        inv_n = 1.0 / RWKV_HEAD
        mean = _seg_sum(y, ones_bd) * inv_n
        yield
        yc = y - mean
        var = _seg_sum(yc * yc, ones_bd) * inv_n
        yield
        yn = yc * lax.rsqrt(var + GN_EPS) * gnw_ref[...] + gnb_ref[...]
        out = (yn + bonus) * gate
        o_ref[blk, :] = out.astype(o_ref.dtype)
        tokens.append(out[sub - 8:sub, 0:LANES])

    def walk_then_tail(base):
        yield from walk(base)
        yield from tail(base)

    def drain(gen):
        for _ in gen:
            pass

    def weave(main, sides, stride):
        sides, finished = list(sides), object()
        for step, _ in enumerate(main):
            if step % stride == 0:
                sides = [g for g in sides if next(g, finished) is not finished]
        for g in sides:
            drain(g)

    nsub = tb // sub
    bases = [sb * sub for sb in (range(nsub - 1, -1, -1) if reverse else range(nsub))]
    drain(prep(bases[0]))
    tokens.clear()
    for idx, base in enumerate(bases):
        sides = []
        if idx > 0:
            sides.append(walk_then_tail(bases[idx - 1]))
        if idx + 1 < nsub:
            sides.append(prep(bases[idx + 1]))
        weave(chains([base]), sides, 1)
    drain(walk_then_tail(bases[-1]))


def _rwkv_masks(tb, chunk, reverse):
    t = np.arange(tb)
    same = (t[:, None] // chunk) == (t[None, :] // chunk)
    before = (t[None, :] >= t[:, None]) if reverse else (t[None, :] <= t[:, None])
    h = np.arange(HALF)

    def bd(row_block, col_block):
        return (h[:, None] // row_block) == (h[None, :] // col_block)

    bf = lambda m: m.astype(np.float32).astype(jnp.bfloat16)
    return [bf(same & before), bf(bd(RWKV_HEAD, RWKV_HEAD)), bf(bd(chunk, RWKV_HEAD)),
            bf(bd(chunk, chunk)), bd(RWKV_HEAD, RWKV_HEAD).astype(np.float32)]


def _rwkv(p, fwd, wl, d, final):
    bsz, seq, _ = p.shape
    tb = min(TB_RWKV, seq)
    nb = seq // tb
    reverse = d == 1
    W = RWKV_WIDTH
    blk = lambda b, j: (b, _time_block(j, nb, reverse), 0)
    const = lambda shape: pl.BlockSpec(shape, lambda b, j: (0,) * len(shape))
    rows = lambda width: pl.BlockSpec((None, tb, width), blk)
    if final:
        yf, zs, kkn = fwd
        in_specs = [rows(RWKV_PAD), rows(W), rows(W)]
        args = [zs, kkn, yf]
    else:
        in_specs = _halo_specs(tb, nb, RWKV_PAD, 0, reverse)
        args = [p, p, p]
    sub = min(SUB_RWKV, tb)
    masks = _rwkv_masks(sub, CHUNK, reverse)
    in_specs += [const(m.shape) for m in masks]
    args += [jnp.asarray(m) for m in masks]
    if final:
        in_specs += [const((1, W)), const((2 * LORA_RANK, W)), const((1, W)), const((2 * LORA_RANK, W)),
                     const((1, W))]
        args += [wl["rwkv_w0"][d], wl["rwkv_w2"][d], wl["rwkv_a0"][d], wl["rwkv_a2"][d], wl["rwkv_k_a"][d]]
        in_specs += [const((1, W)), const((2 * LORA_RANK, W)), const((1, W)), const((GATE_RANK, W)),
                     const((1, W)), const((1, W)), const((1, W))]
        args += [wl["rwkv_a0"][0], wl["rwkv_a2"][0], wl["rwkv_k_a"][0], wl["rwkv_g2"], wl["rwkv_r_k"],
                 wl["rwkv_gn_w"], wl["rwkv_gn_b"]]
        out_specs = rows(W)
        out_shape = jax.ShapeDtypeStruct((bsz, seq, W), BF16)
    else:
        in_specs += [const((1, RWKV_PAD)), const((1, W)), const((2 * LORA_RANK, W)), const((1, W)),
                     const((2 * LORA_RANK, W)), const((1, W)), const((1, W))]
        args += [wl["rwkv_mu"], wl["rwkv_w0"][d], wl["rwkv_w2"][d], wl["rwkv_a0"][d], wl["rwkv_a2"][d],
                 wl["rwkv_k_k"], wl["rwkv_k_a"][d]]
        out_specs = [rows(W), rows(RWKV_PAD), rows(W)]
        out_shape = [jax.ShapeDtypeStruct((bsz, seq, W), F32), jax.ShapeDtypeStruct((bsz, seq, RWKV_PAD), F32),
                     jax.ShapeDtypeStruct((bsz, seq, W), F32)]
    scratch = [pltpu.VMEM((W // HALF, HALF, HALF), F32)]
    scratch += [pltpu.VMEM((tb, W), BF16)] * 8
    scratch += [pltpu.VMEM((tb, W), F32)] * 2
    nhc = (tb // CHUNK) * (W // HALF)
    scratch += [pltpu.VMEM((nhc, HALF, HALF), BF16), pltpu.VMEM((nhc, HALF, HALF), F32)]
    return pl.pallas_call(
        functools.partial(_rwkv_kernel, reverse=reverse, final=final, tb=tb, nb=nb, sub=sub, chunk=CHUNK),
        grid=(bsz, nb),
        in_specs=in_specs,
        out_specs=out_specs,
        out_shape=out_shape,
        scratch_shapes=scratch,
        compiler_params=_params(("arbitrary", "arbitrary")),
        name="rwkv_bwd" if final else "rwkv_fwd",
    )(*args)


def _t5_bucket_table():
    qi = np.arange(ATTN_BLOCK)[:, None]
    kj = np.arange(3 * ATTN_BLOCK)[None, :]
    rel = kj - ATTN_BLOCK - qi
    half = N_BUCKETS // 2
    max_exact = half // 2
    ret = np.where(rel > 0, half, 0)
    n = np.abs(rel)
    nf = np.maximum(n, 1).astype(np.float32)
    large = max_exact + (np.log(nf / max_exact) / math.log(MAX_DISTANCE / max_exact)
                         * (half - max_exact)).astype(np.int32)
    large = np.minimum(large, half - 1)
    return (ret + np.where(n < max_exact, n, large)).astype(np.int32)


def _attn_kernel(q_ref, kp_ref, kc_ref, kn_ref, vp_ref, vc_ref, vn_ref, bkt_ref, rb_ref, sink_ref,
                 o_ref, bias_ref, *, nb):
    b = pl.program_id(0)
    j = pl.program_id(1)
    blk = ATTN_BLOCK

    @pl.when((b == 0) & (j == 0))
    def _():
        bucket = bkt_ref[...]
        qi = lax.broadcasted_iota(jnp.int32, (blk, 3 * blk), 0)
        kj = lax.broadcasted_iota(jnp.int32, (blk, 3 * blk), 1)
        in_window = jnp.abs(kj - blk - qi) <= WINDOW
        for h in range(N_Q_HEADS):
            def body(t, acc, h=h):
                return jnp.where(bucket == t, rb_ref[t, h], acc)
            bias = lax.fori_loop(0, N_BUCKETS, body, jnp.zeros((blk, 3 * blk), F32))
            bias = jnp.where(in_window, bias, NEG)
            bias_ref[0, h] = bias
            bias_ref[1, h] = jnp.where(kj < blk, NEG, bias)
            bias_ref[2, h] = jnp.where(kj >= 2 * blk, NEG, bias)

    variant = jnp.where(j == 0, 1, jnp.where(j == nb - 1, 2, 0))
    hrow = lax.broadcasted_iota(jnp.int32, (GQA_GROUP * blk, 1), 0) // blk
    kcat = jnp.concatenate([kp_ref[...], kc_ref[...], kn_ref[...]], axis=0).astype(BF16)
    vcat = jnp.concatenate([vp_ref[...], vc_ref[...], vn_ref[...]], axis=0).astype(BF16)
    for g in range(N_KV_HEADS):
        qs = jnp.concatenate(
            [q_ref[:, HEAD_DIM * (GQA_GROUP * g + hh):HEAD_DIM * (GQA_GROUP * g + hh + 1)] for hh in range(GQA_GROUP)],
            axis=0).astype(BF16)
        kg = kcat[:, HEAD_DIM * g:HEAD_DIM * (g + 1)]
        vg = vcat[:, HEAD_DIM * g:HEAD_DIM * (g + 1)]
        bias = bias_ref[variant, GQA_GROUP * g:GQA_GROUP * (g + 1)].reshape(GQA_GROUP * blk, 3 * blk)
        s = _dot_nt(qs, kg) + bias
        sk = jnp.zeros((GQA_GROUP * blk, 1), F32)
        for hh in range(GQA_GROUP):
            sk = jnp.where(hrow == hh, sink_ref[0, GQA_GROUP * g + hh], sk)
        m = jnp.maximum(jnp.max(s, axis=-1, keepdims=True), sk)
        pexp = jnp.exp(s - m)
        den = jnp.sum(pexp, axis=-1, keepdims=True) + jnp.exp(sk - m)
        o = _dot(pexp.astype(BF16), vg) / den
        for hh in range(GQA_GROUP):
            h = GQA_GROUP * g + hh
            o_ref[:, HEAD_DIM * h:HEAD_DIM * (h + 1)] = o[blk * hh:blk * (hh + 1)].astype(o_ref.dtype)


def _attn(p, wl):
    bsz, seq, _ = p.shape
    blk = ATTN_BLOCK
    nb = seq // blk
    assert nb >= 2, "the first and the last block of a sequence use different bias variants"
    kcol, vcol = P_K // KV_WIDTH, P_V // KV_WIDTH

    def kv(col, off):
        return pl.BlockSpec((None, blk, KV_WIDTH), lambda b, j: (b, jnp.clip(j + off, 0, nb - 1), col))

    smem = lambda: pl.BlockSpec(memory_space=pltpu.SMEM)
    return pl.pallas_call(
        functools.partial(_attn_kernel, nb=nb),
        grid=(bsz, nb),
        in_specs=[pl.BlockSpec((None, blk, ATTN_WIDTH), lambda b, j: (b, j, P_Q // ATTN_WIDTH)),
                  kv(kcol, -1), kv(kcol, 0), kv(kcol, 1), kv(vcol, -1), kv(vcol, 0), kv(vcol, 1),
                  pl.BlockSpec((blk, 3 * blk), lambda b, j: (0, 0)), smem(), smem()],
        out_specs=pl.BlockSpec((None, blk, ATTN_WIDTH), lambda b, j: (b, j, 0)),
        out_shape=jax.ShapeDtypeStruct((bsz, seq, ATTN_WIDTH), BF16),
        scratch_shapes=[pltpu.VMEM((3, N_Q_HEADS, blk, 3 * blk), F32)],
        compiler_params=_params(("arbitrary", "arbitrary")),
        name="attn",
    )(p, p, p, p, p, p, p, wl["bucket"], wl["rel_bias"], wl["attn_sink"])


def _outproj_kernel(lru_ref, rwkv_ref, attn_ref, x_ref, w_ref, g_ref, o_ref):
    mix = (_dot(lru_ref[...], w_ref[0:LRU_WIDTH, :])
           + _dot(rwkv_ref[...], w_ref[LRU_WIDTH:LRU_WIDTH + RWKV_WIDTH, :])
           + _dot(attn_ref[...], w_ref[LRU_WIDTH + RWKV_WIDTH:, :]))
    ms = jnp.mean(mix * mix, axis=-1, keepdims=True)
    o_ref[...] = x_ref[...] + mix * lax.rsqrt(ms + EPS) * g_ref[...]


def _outproj(lru, rwkv, attn, x2, w, gain):
    t = x2.shape[0]
    tm = min(TM_OUT, t)
    row = lambda width: pl.BlockSpec((tm, width), lambda i: (i, 0))
    return pl.pallas_call(
        _outproj_kernel,
        grid=(t // tm,),
        in_specs=[row(LRU_WIDTH), row(RWKV_WIDTH), row(ATTN_WIDTH), row(D_MODEL),
                  pl.BlockSpec((D_MODEL, D_MODEL), lambda i: (0, 0)),
                  pl.BlockSpec((1, D_MODEL), lambda i: (0, 0))],
        out_specs=row(D_MODEL),
        out_shape=jax.ShapeDtypeStruct((t, D_MODEL), F32),
        compiler_params=_params(("arbitrary",)),
        name="outproj",
    )(lru, rwkv, attn, x2, w, gain)


def _ffn_kernel(x_ref, gpre_ref, wu_ref, wd_ref, gpost_ref, o_ref, h_ref, acc_ref):
    f = pl.program_id(1)

    @pl.when(f == 0)
    def _():
        x = x_ref[...]
        ms = jnp.mean(x * x, axis=-1, keepdims=True)
        h_ref[...] = (x * lax.rsqrt(ms + EPS) * gpre_ref[...]).astype(BF16)
        acc_ref[...] = jnp.zeros_like(acc_ref)

    u = jnp.maximum(_dot(h_ref[...], wu_ref[...]), 0.0)
    acc_ref[...] += _dot((u * u).astype(BF16), wd_ref[...])

    @pl.when(f == pl.num_programs(1) - 1)
    def _():
        y = acc_ref[...]
        ms = jnp.mean(y * y, axis=-1, keepdims=True)
        o_ref[...] = x_ref[...] + y * lax.rsqrt(ms + EPS) * gpost_ref[...]


def _ffn(x2, gpre, wu, wd, gpost):
    t = x2.shape[0]
    tm = min(TM_FFN, t)
    return pl.pallas_call(
        _ffn_kernel,
        grid=(t // tm, D_FF // TF_FFN),
        in_specs=[pl.BlockSpec((tm, D_MODEL), lambda i, f: (i, 0)),
                  pl.BlockSpec((1, D_MODEL), lambda i, f: (0, 0)),
                  pl.BlockSpec((D_MODEL, TF_FFN), lambda i, f: (0, f)),
                  pl.BlockSpec((TF_FFN, D_MODEL), lambda i, f: (f, 0)),
                  pl.BlockSpec((1, D_MODEL), lambda i, f: (0, 0))],
        out_specs=pl.BlockSpec((tm, D_MODEL), lambda i, f: (i, 0)),
        out_shape=jax.ShapeDtypeStruct((t, D_MODEL), F32),
        scratch_shapes=[pltpu.VMEM((tm, D_MODEL), BF16), pltpu.VMEM((tm, D_MODEL), F32)],
        compiler_params=_params(("arbitrary", "arbitrary")),
        name="ffn",
    )(x2, gpre, wu, wd, gpost)


def _block_diag(w):
    n, bi, bj = w.shape
    eye = jnp.eye(n, dtype=w.dtype)
    return (eye[:, None, :, None] * w[:, :, None, :]).reshape(n * bi, n * bj)


def _pad_lora(w2, d):
    z = jnp.zeros_like(w2[d])
    return jnp.concatenate([w2[0], z] if d == 0 else [z, w2[1]], axis=0)


def _layer_weights(l, w):
    row = lambda v: v.reshape(1, -1).astype(F32)
    w_in = w["w_in"][l]
    w_in_p = jnp.concatenate(
        [w_in[:, OFF_RWKV:OFF_ATTN], jnp.zeros((D_MODEL, RWKV_PAD - RWKV_COLS), w_in.dtype),
         w_in[:, 0:OFF_RWKV], w_in[:, OFF_ATTN:OFF_ATTN + ATTN_WIDTH] * (HEAD_DIM ** -0.5),
         w_in[:, OFF_ATTN + ATTN_WIDTH:]], axis=1).astype(BF16)
    mu = jnp.concatenate([w["rwkv_mu"][l], jnp.zeros((RWKV_PAD - RWKV_COLS,), F32)]).reshape(1, -1)
    return {
        "norm_mix_pre": row(w["norm_mix_pre"][l]), "norm_mix_post": row(w["norm_mix_post"][l]),
        "norm_ffn_pre": row(w["norm_ffn_pre"][l]), "norm_ffn_post": row(w["norm_ffn_post"][l]),
        "w_in": w_in_p, "w_out": w["w_out"][l].astype(BF16),
        "conv_w": w["conv_w"][l].astype(F32), "conv_b": row(w["conv_b"][l]),
        "lru_wa": [_block_diag(w["lru_wa"][l, d]).astype(BF16) for d in range(2)],
        "lru_wx": [_block_diag(w["lru_wx"][l, d]).astype(BF16) for d in range(2)],
        "lru_ba": [row(w["lru_ba"][l, d]) for d in range(2)],
        "lru_bx": [row(w["lru_bx"][l, d]) for d in range(2)],
        "lru_lambda": [row(w["lru_lambda"][l, d]) for d in range(2)],
        "rwkv_mu": mu,
        "rwkv_w0": [row(w["rwkv_w0"][l, d]) for d in range(2)],
        "rwkv_w2": [_pad_lora(w["rwkv_w2"][l], d).astype(BF16) for d in range(2)],
        "rwkv_a0": [row(w["rwkv_a0"][l, d]) for d in range(2)],
        "rwkv_a2": [_pad_lora(w["rwkv_a2"][l], d).astype(BF16) for d in range(2)],
        "rwkv_g2": w["rwkv_g2"][l].astype(BF16),
        "rwkv_k_k": row(w["rwkv_k_k"][l]),
        "rwkv_k_a": [row(w["rwkv_k_a"][l, d]) for d in range(2)],
        "rwkv_r_k": row(w["rwkv_r_k"][l]),
        "rwkv_gn_w": row(w["rwkv_gn_w"][l]), "rwkv_gn_b": row(w["rwkv_gn_b"][l]),
        "attn_sink": w["attn_sink"][l].reshape(1, -1).astype(F32),
        "rel_bias": w["rel_bias"].astype(F32),
        "bucket": jnp.asarray(_t5_bucket_table()),
        "w_up": w["w_up"][l].astype(BF16), "w_down": w["w_down"][l].astype(BF16),
    }


def _trunk(x, layers):
    bsz, seq, _ = x.shape
    t = bsz * seq
    x2 = x.reshape(t, D_MODEL)
    for wl in layers:
        p = _inproj(x2, wl["norm_mix_pre"], wl["w_in"]).reshape(bsz, seq, P_COLS)
        hf = _lru(p, None, wl, 0, False)
        lru_out = _lru(p, hf, wl, 1, True)
        rwkv_fwd = _rwkv(p, None, wl, 0, False)
        rwkv_out = _rwkv(p, rwkv_fwd, wl, 1, True)
        attn_out = _attn(p, wl)
        x2 = _outproj(lru_out.reshape(t, LRU_WIDTH), rwkv_out.reshape(t, RWKV_WIDTH),
                      attn_out.reshape(t, ATTN_WIDTH), x2, wl["w_out"], wl["norm_mix_post"])
        x2 = _ffn(x2, wl["norm_ffn_pre"], wl["w_up"], wl["w_down"], wl["norm_ffn_post"])
    return x2.reshape(bsz, seq, D_MODEL)


def kernel(x_prompt, x_sample, norm_mix_pre, norm_mix_post, norm_ffn_pre, norm_ffn_post, w_in, w_out, conv_w, conv_b, lru_wa, lru_ba, lru_wx, lru_bx, lru_lambda, rwkv_mu, rwkv_w0, rwkv_w2, rwkv_a0, rwkv_a2, rwkv_g2, rwkv_k_k, rwkv_k_a, rwkv_r_k, rwkv_gn_w, rwkv_gn_b, attn_sink, rel_bias, w_up, w_down):
    w = dict(norm_mix_pre=norm_mix_pre, norm_mix_post=norm_mix_post, norm_ffn_pre=norm_ffn_pre,
             norm_ffn_post=norm_ffn_post, w_in=w_in, w_out=w_out, conv_w=conv_w, conv_b=conv_b,
             lru_wa=lru_wa, lru_ba=lru_ba, lru_wx=lru_wx, lru_bx=lru_bx, lru_lambda=lru_lambda,
             rwkv_mu=rwkv_mu, rwkv_w0=rwkv_w0, rwkv_w2=rwkv_w2, rwkv_a0=rwkv_a0, rwkv_a2=rwkv_a2,
             rwkv_g2=rwkv_g2, rwkv_k_k=rwkv_k_k, rwkv_k_a=rwkv_k_a, rwkv_r_k=rwkv_r_k,
             rwkv_gn_w=rwkv_gn_w, rwkv_gn_b=rwkv_gn_b, attn_sink=attn_sink, rel_bias=rel_bias,
             w_up=w_up, w_down=w_down)
    layers = [_layer_weights(l, w) for l in range(w_in.shape[0])]
    return (_trunk(x_prompt, layers), _trunk(x_sample, layers))
```

```python
import functools
import math

import numpy as np
import jax
import jax.numpy as jnp
from jax import lax
from jax.experimental import pallas as pl
from jax.experimental.pallas import tpu as pltpu

F32 = jnp.float32
BF16 = jnp.bfloat16

D_MODEL = 2048
LRU_WIDTH = 512
LRU_BLOCKS = 8
LRU_C = 8.0
RWKV_WIDTH = 512
RWKV_HEAD = 64
RWKV_HEADS = 8
LORA_RANK = 64
GATE_RANK = 128
DECAY_SCALE = 0.606531
GN_EPS = 64e-5
ATTN_WIDTH = 1024
HEAD_DIM = 128
N_Q_HEADS = 8
N_KV_HEADS = 2
GQA_GROUP = 4
KV_WIDTH = 256
WINDOW = 128
ATTN_BLOCK = 128
N_BUCKETS = 32
MAX_DISTANCE = 128
D_FF = 8192
EPS = 1e-6
OFF_RWKV = 2 * LRU_WIDTH
RWKV_COLS = 3 * RWKV_WIDTH + 4 * LORA_RANK + GATE_RANK
OFF_ATTN = OFF_RWKV + RWKV_COLS

RWKV_PAD = 2048
P_LRU_X = 2048
P_LRU_G = 2560
P_Q = 3072
P_K = 4096
P_V = 4352
P_COLS = 4608

LANES = 128
HALF = 256
VMEM_LIMIT = 56 * 1024 * 1024

TM_PROJ = 1024
TN_PROJ = 1536
TM_OUT = 512
TM_FFN = 512
TF_FFN = 1024
TB_LRU = 512
ATTN_QBLOCKS = 2
TB_RWKV = 512
SUB_RWKV = 256
CHUNK = 64
NEG = -1e30


def _dot(a, b):
    return jnp.dot(a, b, preferred_element_type=F32)


def _dot_nt(a, b):
    return lax.dot_general(a, b, (((1,), (1,)), ((), ())), preferred_element_type=F32)


def _dot_tn(a, b):
    return lax.dot_general(a, b, (((0,), (0,)), ((), ())), preferred_element_type=F32)


def _split2(x):
    hi = x.astype(BF16)
    lo = (x - hi.astype(F32)).astype(BF16)
    return hi, lo


def _params(sem):
    return pltpu.CompilerParams(dimension_semantics=sem, vmem_limit_bytes=VMEM_LIMIT)


def _inproj_kernel(x_ref, g_ref, w_ref, o_ref, h_ref):
    @pl.when(pl.program_id(1) == 0)
    def _():
        x = x_ref[...]
        ms = jnp.mean(x * x, axis=-1, keepdims=True)
        h_ref[...] = (x * lax.rsqrt(ms + EPS) * g_ref[...]).astype(BF16)

    o_ref[...] = _dot(h_ref[...], w_ref[...])


def _inproj(x2, gain, w):
    t = x2.shape[0]
    tm = min(TM_PROJ, t)
    return pl.pallas_call(
        _inproj_kernel,
        grid=(t // tm, P_COLS // TN_PROJ),
        in_specs=[
            pl.BlockSpec((tm, D_MODEL), lambda i, j: (i, 0)),
            pl.BlockSpec((1, D_MODEL), lambda i, j: (0, 0)),
            pl.BlockSpec((D_MODEL, TN_PROJ), lambda i, j: (0, j)),
        ],
        out_specs=pl.BlockSpec((tm, TN_PROJ), lambda i, j: (i, j)),
        out_shape=jax.ShapeDtypeStruct((t, P_COLS), F32),
        scratch_shapes=[pltpu.VMEM((tm, D_MODEL), BF16)],
        compiler_params=_params(("arbitrary", "arbitrary")),
        name="inproj",
    )(x2, gain, w)


def _time_block(j, nb, reverse):
    return nb - 1 - j if reverse else j


def _halo_specs(tb, nb, width, col, reverse):
    rows8 = tb // 8
    last8 = nb * rows8 - 1

    def blk(b, j):
        return (b, _time_block(j, nb, reverse), col)

    def prev(b, j):
        return (b, jnp.maximum(_time_block(j, nb, reverse) * rows8 - 1, 0), col)

    def nxt(b, j):
        return (b, jnp.minimum((_time_block(j, nb, reverse) + 1) * rows8, last8), col)

    return [pl.BlockSpec((None, tb, width), blk),
            pl.BlockSpec((None, 8, width), prev),
            pl.BlockSpec((None, 8, width), nxt)]


def _shift_time(x3, halo, back):
    r8 = lax.broadcasted_iota(jnp.int32, (1, 8, 1), 1)
    shift = back % 8
    w = pltpu.roll(x3, shift, 1)
    wh = pltpu.roll(halo, shift, 0)[None]
    if back > 0:
        return jnp.where(r8 < back, jnp.concatenate([wh, w[:-1]], axis=0), w)
    return jnp.where(r8 >= 8 + back, jnp.concatenate([w[1:], wh], axis=0), w)


def _softplus(z):
    return jnp.maximum(z, 0.0) + jnp.log(1.0 + jnp.exp(-jnp.abs(z)))


def _lru_kernel(*refs, reverse, final, tb, nb, seq):
    if final:
        xc_ref, g_ref, hf_ref, wa_ref, wx_ref, ba_ref, bx_ref, lam_ref, o_ref, carry_ref = refs
    else:
        (x_ref, xp_ref, xn_ref, cw_ref, cb_ref, wa_ref, wx_ref, ba_ref, bx_ref, lam_ref,
         o_ref, xc_ref, carry_ref) = refs
    j = pl.program_id(1)
    jj = _time_block(j, nb, reverse)

    @pl.when(j == 0)
    def _():
        carry_ref[...] = jnp.zeros_like(carry_ref)

    ngroups = tb // 8
    width = xc_ref.shape[1]
    row = lax.broadcasted_iota(jnp.int32, (tb, 1), 0)
    if final:
        xc = xc_ref[...]
    else:
        x3 = x_ref[...].reshape(ngroups, 8, width)
        prev8 = jnp.where(jj > 0, xp_ref[...], 0.0)
        next8 = jnp.where(jj < nb - 1, xn_ref[...], 0.0)
        cw = cw_ref[...]
        xc = (cw[0:1] * _shift_time(x3, prev8, 2) + cw[1:2] * _shift_time(x3, prev8, 1) + cw[2:3] * x3
              + cw[3:4] * _shift_time(x3, next8, -1) + cb_ref[...]).reshape(tb, width)
        xc_ref[...] = xc

    xcb = xc.astype(BF16)
    r = jax.nn.sigmoid(_dot(xcb, wa_ref[...]) + ba_ref[...])
    i = jax.nn.sigmoid(_dot(xcb, wx_ref[...]) + bx_ref[...])
    a = jnp.exp2(r * ((-LRU_C * math.log2(math.e)) * _softplus(-lam_ref[...])))
    om = 1.0 - a * a
    mult = jnp.where(om > 0.0, om * lax.rsqrt(om), 0.0)
    gpos = jj * tb + row
    first = (gpos == seq - 1) if reverse else (gpos == 0)
    mult = jnp.where(first, 1.0, mult)
    u = mult * i * xc

    r8 = lax.broadcasted_iota(jnp.int32, (1, 8, 1), 1)
    acc_a, acc_u = a.reshape(ngroups, 8, width), u.reshape(ngroups, 8, width)
    for s in (1, 2, 4):
        shift, valid = (8 - s, r8 < 8 - s) if reverse else (s, r8 >= s)
        sh_a, sh_u = pltpu.roll(acc_a, shift, 1), pltpu.roll(acc_u, shift, 1)
        acc_u = acc_a * jnp.where(valid, sh_u, 0.0) + acc_u
        acc_a = acc_a * jnp.where(valid, sh_a, 1.0)
    acc_a, acc_u = acc_a.reshape(tb, width), acc_u.reshape(tb, width)
    last_row = 0 if reverse else 7
    carry = carry_ref[...]
    carries = [None] * ngroups
    for gi in (range(ngroups - 1, -1, -1) if reverse else range(ngroups)):
        carries[gi] = carry
        at = 8 * gi + last_row
        carry = (jnp.broadcast_to(acc_a[at:at + 1], carry.shape) * carry
                 + jnp.broadcast_to(acc_u[at:at + 1], carry.shape))
    carry_ref[...] = carry
    h = acc_a * jnp.concatenate(carries, axis=0) + acc_u

    if final:
        o_ref[...] = ((hf_ref[...] + h) * jax.nn.gelu(g_ref[...], approximate=True)).astype(o_ref.dtype)
    else:
        o_ref[...] = h


def _lru(p, fwd, wl, d, final):
    bsz, seq, _ = p.shape
    tb = min(TB_LRU, seq)
    nb = seq // tb
    reverse = d == 1
    blk = lambda b, j: (b, _time_block(j, nb, reverse), 0)
    rows = pl.BlockSpec((None, tb, LRU_WIDTH), blk)
    const = lambda shape: pl.BlockSpec(shape, lambda b, j: (0,) * len(shape))
    if final:
        hf, xc = fwd
        in_specs = [rows, pl.BlockSpec((None, tb, LRU_WIDTH),
                                       lambda b, j: (b, _time_block(j, nb, reverse), P_LRU_G // LRU_WIDTH)), rows]
        args = [xc, p, hf]
        out_specs = rows
        out_shape = jax.ShapeDtypeStruct((bsz, seq, LRU_WIDTH), BF16)
    else:
        in_specs = _halo_specs(tb, nb, LRU_WIDTH, P_LRU_X // LRU_WIDTH, reverse)
        in_specs += [const((4, LRU_WIDTH)), const((1, LRU_WIDTH))]
        args = [p, p, p, wl["conv_w"], wl["conv_b"]]
        out_specs = [rows, rows]
        out_shape = [jax.ShapeDtypeStruct((bsz, seq, LRU_WIDTH), F32)] * 2
    in_specs += [const((LRU_WIDTH, LRU_WIDTH)), const((LRU_WIDTH, LRU_WIDTH)), const((1, LRU_WIDTH)),
                 const((1, LRU_WIDTH)), const((1, LRU_WIDTH))]
    args += [wl["lru_wa"][d], wl["lru_wx"][d], wl["lru_ba"][d], wl["lru_bx"][d], wl["lru_lambda"][d]]
    return pl.pallas_call(
        functools.partial(_lru_kernel, reverse=reverse, final=final, tb=tb, nb=nb, seq=seq),
        grid=(bsz, nb),
        in_specs=in_specs,
        out_specs=out_specs,
        out_shape=out_shape,
        scratch_shapes=[pltpu.VMEM((8, LRU_WIDTH), F32)],
        compiler_params=_params(("arbitrary", "arbitrary")),
        name="lru_bwd" if final else "lru_fwd",
    )(*args)


def _seg_sum(x, ones_bd):
    xb = x.astype(BF16)
    return jnp.concatenate(
        [_dot(xb[:, HALF * hf:HALF * (hf + 1)], ones_bd) for hf in range(x.shape[1] // HALF)], axis=1)


def _zero_of(x):
    bits = pltpu.bitcast(x, jnp.uint32)
    return ((bits >> 16) >> 16).astype(F32)


def _rwkv_kernel(*refs, reverse, final, tb, nb, sub, chunk):
    if final:
        (zs_ref, kkn_ref, yf_ref, tri_ref, ones_ref, hm_ref, bdm_ref, stm_ref,
         w0_ref, w2_ref, a0_ref, a2_ref, ka_ref,
         a0f_ref, a2f_ref, kaf_ref, g2_ref, rk_ref, gnw_ref, gnb_ref,
         o_ref, s_ref, qk_s, rt_s, kt_s, bt_s, ktg_s, btg_s, v_s, rp_s, egl_s, y_s, en_s, f_s) = refs
    else:
        (z_ref, zp_ref, zn_ref, tri_ref, ones_ref, hm_ref, bdm_ref, stm_ref,
         mu_ref, w0_ref, w2_ref, a0_ref, a2_ref, kk_ref, ka_ref,
         o_ref, zs_ref, kkn_ref, s_ref, qk_s, rt_s, kt_s, bt_s, ktg_s, btg_s, v_s, rp_s, egl_s, y_s, en_s,
         f_s) = refs
    j = pl.program_id(1)
    jj = _time_block(j, nb, reverse)
    W = RWKV_WIDTH
    group = HALF // chunk
    halves = W // HALF
    nchunk = sub // chunk
    n_levels = int(math.log2(chunk))
    order = range(nchunk - 1, -1, -1) if reverse else range(nchunk)

    @pl.when(j == 0)
    def _():
        s_ref[...] = jnp.zeros_like(s_ref)

    ones_bd = ones_ref[...]
    tri = tri_ref[...]
    head_mask = hm_ref[...]
    bd_mask = bdm_ref[...]
    st_mask = stm_ref[...]
    wi = lax.broadcasted_iota(jnp.int32, (chunk, HALF), 0)
    wj = lax.broadcasted_iota(jnp.int32, (chunk, HALF), 1) % chunk
    strict = (wj > wi) if reverse else (wj < wi)
    incl = (wj >= wi) if reverse else (wj <= wi)
    eye_w = jnp.where(wi == wj, 1.0, 0.0)

    def stack(xb):
        return jnp.concatenate([xb] * group, axis=0) * head_mask

    def blockdiag(xw):
        return jnp.concatenate([xw.astype(BF16)] * group, axis=0) * bd_mask

    def lanes(hf):
        return slice(HALF * hf, HALF * hf + HALF)

    saved = {}
    tokens = []

    def prep(base):
        blk = slice(base, base + sub)
        if final:
            zs = zs_ref[blk, :]
        else:
            z3 = z_ref[blk, :].reshape(sub // 8, 8, RWKV_PAD)
            prev8 = jnp.where(jj > 0, zp_ref[...], 0.0) if base == 0 else z_ref[base - 8:base, :]
            next8 = (jnp.where(jj < nb - 1, zn_ref[...], 0.0) if base + sub == tb
                     else z_ref[base + sub:base + sub + 8, :])
            neigh = _shift_time(z3, prev8, 1) + _shift_time(z3, next8, -1)
            yield
            zs = (z3 + (0.5 * neigh - z3) * mu_ref[...]).reshape(sub, RWKV_PAD)
            zs_ref[blk, :] = zs
        yield
        r = zs[:, 0:W]
        k = zs[:, W:2 * W]
        v = zs[:, 2 * W:3 * W]
        dw = zs[:, 3 * W:3 * W + 2 * LORA_RANK]
        da = zs[:, 3 * W + 2 * LORA_RANK:3 * W + 4 * LORA_RANK]
        dg = zs[:, 3 * W + 4 * LORA_RANK:3 * W + 4 * LORA_RANK + GATE_RANK]

        lw = -DECAY_SCALE * jax.nn.sigmoid(_dot(jnp.tanh(dw).astype(BF16), w2_ref[...]) + w0_ref[...])
        yield
        dab = da.astype(BF16)
        a = jax.nn.sigmoid(_dot(dab, a2_ref[...]) + a0_ref[...])
        kdir = k * (1.0 + (a - 1.0) * ka_ref[...])
        yield
        if final:
            kk = kkn_ref[blk, :]
        else:
            kk = k * kk_ref[...]
            kk = kk * lax.rsqrt(jnp.maximum(_seg_sum(kk * kk, ones_bd), 1e-24))
            kkn_ref[blk, :] = kk
        b = kk * a
        yield

        hi, lo = _split2(lw)
        g = _dot(tri, hi) + _dot(tri, lo)
        ends = [(c if reverse else c + 1) * chunk - (0 if reverse else 1) for c in range(nchunk)]
        gl = jnp.concatenate([jnp.broadcast_to(g[e:e + 1], (chunk, W)) for e in ends], axis=0)
        yield
        e_g = jnp.exp(g)
        e_ng = jnp.exp(-g)
        e_gl = jnp.exp(gl)
        yield
        kt = kdir * e_ng
        bt = b * e_ng
        qk_s[blk, :] = (kk * jnp.exp(g - lw)).astype(BF16)
        rt_s[blk, :] = (r * e_g).astype(BF16)
        yield
        kt_s[blk, :] = kt.astype(BF16)
        bt_s[blk, :] = bt.astype(BF16)
        ktg_s[blk, :] = (kt * e_gl).astype(BF16)
        yield
        btg_s[blk, :] = (-(bt * e_gl)).astype(BF16)
        v_s[blk, :] = v.astype(BF16)
        egl_s[blk, :] = e_gl
        tokens.append(e_gl[0:8, 0:LANES])
        if final:
            yield
            a_f = jax.nn.sigmoid(_dot(dab, a2f_ref[...]) + a0f_ref[...])
            kdir_f = k * (1.0 + (a_f - 1.0) * kaf_ref[...])
            bonus = _seg_sum(r * (kdir_f + kdir) * rk_ref[...], ones_bd) * v
            yield
            gate = _dot(jax.nn.sigmoid(dg).astype(BF16), g2_ref[...])
            saved[base] = (bonus, gate)
            tokens.append(gate[0:8, 0:LANES] + bonus[0:8, 0:LANES])

    def chains(bases):
        ids = [(base // chunk + c, hf) for base in bases for c in order for hf in range(halves)]
        n = len(ids)

        def rows(c):
            return slice(c * chunk, (c + 1) * chunk)

        def slot(c, hf):
            return c * halves + hf

        a_k, b_k, b_b, pw, inv = [], [], [], [], []
        for c, hf in ids:
            sl, ls = rows(c), lanes(hf)
            x2 = jnp.concatenate([qk_s[sl, ls], rt_s[sl, ls]], axis=0)
            gk = _dot_nt(x2, stack(kt_s[sl, ls]))
            gb = _dot_nt(x2, stack(bt_s[sl, ls]))
            a_k.append(jnp.where(strict, gk[:chunk], 0.0).astype(BF16))
            b_k.append(jnp.where(incl, gk[chunk:], 0.0).astype(BF16))
            b_b.append(jnp.where(incl, gb[chunk:], 0.0).astype(BF16))
            inv.append(jnp.where(strict, -gb[:chunk], 0.0))
            yield
        for i in range(n):
            pw.append(_dot(inv[i].astype(BF16), blockdiag(inv[i])))
            inv[i] = eye_w + inv[i]
            yield
        for level in range(1, n_levels):
            for i in range(n):
                bdp = blockdiag(pw[i])
                if level < n_levels - 1:
                    both = _dot(jnp.concatenate([pw[i], inv[i]], axis=0).astype(BF16), bdp)
                    pw[i] = both[:chunk]
                    inv[i] = inv[i] + both[chunk:]
                else:
                    inv[i] = inv[i] + _dot(inv[i].astype(BF16), bdp)
                yield
        tw = [t.astype(BF16) for t in inv]
        vm, akv, qp, vp = [], [], [], []
        for i, (c, hf) in enumerate(ids):
            vm.append(stack(v_s[rows(c), lanes(hf)]))
            akv.append(_dot(a_k[i], vm[i]))
            yield
        for i, (c, hf) in enumerate(ids):
            qp.append(_dot(tw[i], stack(qk_s[rows(c), lanes(hf)])).astype(BF16))
            yield
        for i in range(n):
            vp.append(_dot(tw[i], stack(akv[i].astype(BF16))).astype(BF16))
            yield
        for i, (c, hf) in enumerate(ids):
            sl, ls = rows(c), lanes(hf)
            rp_s[sl, ls] = (rt_s[sl, ls].astype(F32) - _dot(b_b[i], stack(qp[i]))).astype(BF16)
            yield
        for i, (c, hf) in enumerate(ids):
            y_s[rows(c), lanes(hf)] = _dot(b_k[i], vm[i]) - _dot(b_b[i], stack(vp[i]))
            yield
        for i, (c, hf) in enumerate(ids):
            en = _dot_tn(qp[i], btg_s[rows(c), lanes(hf)])
            en_s[slot(c, hf)] = (en * st_mask).astype(BF16)
            yield
        for i, (c, hf) in enumerate(ids):
            sl, ls = rows(c), lanes(hf)
            fm = _dot_tn(jnp.concatenate([v_s[sl, ls], vp[i]], axis=0),
                         jnp.concatenate([ktg_s[sl, ls], btg_s[sl, ls]], axis=0))
            fm = fm * st_mask
            if i == n - 1:
                while tokens:
                    fm = fm + _zero_of(tokens.pop())[0:1, 0:1]
            f_s[slot(c, hf)] = fm
            yield

    def walk(base):
        for c in order:
            sl = slice(base + c * chunk, base + (c + 1) * chunk)
            for hf in range(halves):
                ls = lanes(hf)
                sidx = (base // chunk + c) * halves + hf
                st = s_ref[hf]
                stb = st.astype(BF16)
                y_s[sl, ls] = y_s[sl, ls] + _dot_nt(rp_s[sl, ls], stb)
                decay = egl_s[sl, ls][0:1]
                s_ref[hf] = st * decay + _dot(stb, en_s[sidx]) + f_s[sidx]
            yield

    def tail(base):
        blk = slice(base, base + sub)
        if not final:
            y = y_s[blk, :]
            o_ref[blk, :] = y
            tokens.append(y[0:8, 0:LANES])
            return
        bonus, gate = saved.pop(base)
        y = yf_ref[blk, :] + y_s[blk, :]
        inv_n = 1.0 / RWKV_HEAD
        mean = _seg_sum(y, ones_bd) * inv_n
        yield
        yc = y - mean
        var = _seg_sum(yc * yc, ones_bd) * inv_n
        yield
        yn = yc * lax.rsqrt(var + GN_EPS) * gnw_ref[...] + gnb_ref[...]
        out = (yn + bonus) * gate
        o_ref[blk, :] = out.astype(o_ref.dtype)
        tokens.append(out[sub - 8:sub, 0:LANES])

    def walk_then_tail(base):
        yield from walk(base)
        yield from tail(base)

    def drain(gen):
        for _ in gen:
            pass

    def weave(main, sides, stride):
        sides, finished = list(sides), object()
        for step, _ in enumerate(main):
            if step % stride == 0:
                sides = [g for g in sides if next(g, finished) is not finished]
        for g in sides:
            drain(g)

    nsub = tb // sub
    bases = [sb * sub for sb in (range(nsub - 1, -1, -1) if reverse else range(nsub))]
    drain(prep(bases[0]))
    tokens.clear()
    for idx, base in enumerate(bases):
        sides = []
        if idx > 0:
            sides.append(walk_then_tail(bases[idx - 1]))
        if idx + 1 < nsub:
            sides.append(prep(bases[idx + 1]))
        weave(chains([base]), sides, 1)
    drain(walk_then_tail(bases[-1]))


def _rwkv_masks(tb, chunk, reverse):
    t = np.arange(tb)
    same = (t[:, None] // chunk) == (t[None, :] // chunk)
    before = (t[None, :] >= t[:, None]) if reverse else (t[None, :] <= t[:, None])
    h = np.arange(HALF)

    def bd(row_block, col_block):
        return (h[:, None] // row_block) == (h[None, :] // col_block)

    bf = lambda m: m.astype(np.float32).astype(jnp.bfloat16)
    return [bf(same & before), bf(bd(RWKV_HEAD, RWKV_HEAD)), bf(bd(chunk, RWKV_HEAD)),
            bf(bd(chunk, chunk)), bd(RWKV_HEAD, RWKV_HEAD).astype(np.float32)]


def _rwkv(p, fwd, wl, d, final):
    bsz, seq, _ = p.shape
    tb = min(TB_RWKV, seq)
    nb = seq // tb
    reverse = d == 1
    W = RWKV_WIDTH
    blk = lambda b, j: (b, _time_block(j, nb, reverse), 0)
    const = lambda shape: pl.BlockSpec(shape, lambda b, j: (0,) * len(shape))
    rows = lambda width: pl.BlockSpec((None, tb, width), blk)
    if final:
        yf, zs, kkn = fwd
        in_specs = [rows(RWKV_PAD), rows(W), rows(W)]
        args = [zs, kkn, yf]
    else:
        in_specs = _halo_specs(tb, nb, RWKV_PAD, 0, reverse)
        args = [p, p, p]
    sub = min(SUB_RWKV, tb)
    masks = _rwkv_masks(sub, CHUNK, reverse)
    in_specs += [const(m.shape) for m in masks]
    args += [jnp.asarray(m) for m in masks]
    if final:
        in_specs += [const((1, W)), const((2 * LORA_RANK, W)), const((1, W)), const((2 * LORA_RANK, W)),
                     const((1, W))]
        args += [wl["rwkv_w0"][d], wl["rwkv_w2"][d], wl["rwkv_a0"][d], wl["rwkv_a2"][d], wl["rwkv_k_a"][d]]
        in_specs += [const((1, W)), const((2 * LORA_RANK, W)), const((1, W)), const((GATE_RANK, W)),
                     const((1, W)), const((1, W)), const((1, W))]
        args += [wl["rwkv_a0"][0], wl["rwkv_a2"][0], wl["rwkv_k_a"][0], wl["rwkv_g2"], wl["rwkv_r_k"],
                 wl["rwkv_gn_w"], wl["rwkv_gn_b"]]
        out_specs = rows(W)
        out_shape = jax.ShapeDtypeStruct((bsz, seq, W), BF16)
    else:
        in_specs += [const((1, RWKV_PAD)), const((1, W)), const((2 * LORA_RANK, W)), const((1, W)),
                     const((2 * LORA_RANK, W)), const((1, W)), const((1, W))]
        args += [wl["rwkv_mu"], wl["rwkv_w0"][d], wl["rwkv_w2"][d], wl["rwkv_a0"][d], wl["rwkv_a2"][d],
                 wl["rwkv_k_k"], wl["rwkv_k_a"][d]]
        out_specs = [rows(W), rows(RWKV_PAD), rows(W)]
        out_shape = [jax.ShapeDtypeStruct((bsz, seq, W), F32), jax.ShapeDtypeStruct((bsz, seq, RWKV_PAD), F32),
                     jax.ShapeDtypeStruct((bsz, seq, W), F32)]
    scratch = [pltpu.VMEM((W // HALF, HALF, HALF), F32)]
    scratch += [pltpu.VMEM((tb, W), BF16)] * 8
    scratch += [pltpu.VMEM((tb, W), F32)] * 2
    nhc = (tb // CHUNK) * (W // HALF)
    scratch += [pltpu.VMEM((nhc, HALF, HALF), BF16), pltpu.VMEM((nhc, HALF, HALF), F32)]
    return pl.pallas_call(
        functools.partial(_rwkv_kernel, reverse=reverse, final=final, tb=tb, nb=nb, sub=sub, chunk=CHUNK),
        grid=(bsz, nb),
        in_specs=in_specs,
        out_specs=out_specs,
        out_shape=out_shape,
        scratch_shapes=scratch,
        compiler_params=_params(("arbitrary", "arbitrary")),
        name="rwkv_bwd" if final else "rwkv_fwd",
    )(*args)


def _t5_bucket_table():
    qi = np.arange(ATTN_BLOCK)[:, None]
    kj = np.arange(3 * ATTN_BLOCK)[None, :]
    rel = kj - ATTN_BLOCK - qi
    half = N_BUCKETS // 2
    max_exact = half // 2
    ret = np.where(rel > 0, half, 0)
    n = np.abs(rel)
    nf = np.maximum(n, 1).astype(np.float32)
    large = max_exact + (np.log(nf / max_exact) / math.log(MAX_DISTANCE / max_exact)
                         * (half - max_exact)).astype(np.int32)
    large = np.minimum(large, half - 1)
    return (ret + np.where(n < max_exact, n, large)).astype(np.int32)


def _attn_kernel(*refs, nb):
    nkv = ATTN_QBLOCKS + 2
    q_ref, k_refs, v_refs = refs[0], refs[1:1 + nkv], refs[1 + nkv:1 + 2 * nkv]
    bkt_ref, rb_ref, sink_ref, o_ref, bias_ref = refs[1 + 2 * nkv:]
    b = pl.program_id(0)
    j = pl.program_id(1)
    blk = ATTN_BLOCK

    @pl.when((b == 0) & (j == 0))
    def _():
        bucket = bkt_ref[...]
        qi = lax.broadcasted_iota(jnp.int32, (blk, 3 * blk), 0)
        kj = lax.broadcasted_iota(jnp.int32, (blk, 3 * blk), 1)
        in_window = jnp.abs(kj - blk - qi) <= WINDOW
        for h in range(N_Q_HEADS):
            def body(t, acc, h=h):
                return jnp.where(bucket == t, rb_ref[t, h], acc)
            bias = lax.fori_loop(0, N_BUCKETS, body, jnp.zeros((blk, 3 * blk), F32))
            bias = jnp.where(in_window, bias, NEG)
            bias_ref[0, h] = bias
            bias_ref[1, h] = jnp.where(kj < blk, NEG, bias)
            bias_ref[2, h] = jnp.where(kj >= 2 * blk, NEG, bias)

    hrow = lax.broadcasted_iota(jnp.int32, (GQA_GROUP * blk, 1), 0) // blk
    kblocks = [r[...].astype(BF16) for r in k_refs]
    vblocks = [r[...].astype(BF16) for r in v_refs]
    for qb in range(ATTN_QBLOCKS):
        first = (j == 0) if qb == 0 else False
        last = (j == nb - 1) if qb == ATTN_QBLOCKS - 1 else False
        variant = jnp.where(first, 1, jnp.where(last, 2, 0))
        kcat = jnp.concatenate(kblocks[qb:qb + 3], axis=0)
        vcat = jnp.concatenate(vblocks[qb:qb + 3], axis=0)
        qrows = slice(blk * qb, blk * (qb + 1))
        for g in range(N_KV_HEADS):
            qs = jnp.concatenate(
                [q_ref[qrows, HEAD_DIM * (GQA_GROUP * g + hh):HEAD_DIM * (GQA_GROUP * g + hh + 1)]
                 for hh in range(GQA_GROUP)], axis=0).astype(BF16)
            kg = kcat[:, HEAD_DIM * g:HEAD_DIM * (g + 1)]
            vg = vcat[:, HEAD_DIM * g:HEAD_DIM * (g + 1)]
            bias = bias_ref[variant, GQA_GROUP * g:GQA_GROUP * (g + 1)].reshape(GQA_GROUP * blk, 3 * blk)
            s = _dot_nt(qs, kg) + bias
            sk = jnp.zeros((GQA_GROUP * blk, 1), F32)
            for hh in range(GQA_GROUP):
                sk = jnp.where(hrow == hh, sink_ref[0, GQA_GROUP * g + hh], sk)
            m = jnp.maximum(jnp.max(s, axis=-1, keepdims=True), sk)
            pexp = jnp.exp(s - m)
            den = jnp.sum(pexp, axis=-1, keepdims=True) + jnp.exp(sk - m)
            o = _dot(pexp.astype(BF16), vg) / den
            for hh in range(GQA_GROUP):
                h = GQA_GROUP * g + hh
                o_ref[qrows, HEAD_DIM * h:HEAD_DIM * (h + 1)] = o[blk * hh:blk * (hh + 1)].astype(o_ref.dtype)


def _attn(p, wl):
    bsz, seq, _ = p.shape
    blk = ATTN_BLOCK
    qrows = ATTN_QBLOCKS * blk
    nb = seq // qrows
    nkb = seq // blk
    assert nkb >= 2, "the first and the last block of a sequence use different bias variants"
    kcol, vcol = P_K // KV_WIDTH, P_V // KV_WIDTH

    def kv(col, off):
        return pl.BlockSpec((None, blk, KV_WIDTH),
                            lambda b, j: (b, jnp.clip(ATTN_QBLOCKS * j + off, 0, nkb - 1), col))

    offs = range(-1, ATTN_QBLOCKS + 1)
    smem = lambda: pl.BlockSpec(memory_space=pltpu.SMEM)
    return pl.pallas_call(
        functools.partial(_attn_kernel, nb=nb),
        grid=(bsz, nb),
        in_specs=[pl.BlockSpec((None, qrows, ATTN_WIDTH), lambda b, j: (b, j, P_Q // ATTN_WIDTH))]
                 + [kv(kcol, o) for o in offs] + [kv(vcol, o) for o in offs]
                 + [pl.BlockSpec((blk, 3 * blk), lambda b, j: (0, 0)), smem(), smem()],
        out_specs=pl.BlockSpec((None, qrows, ATTN_WIDTH), lambda b, j: (b, j, 0)),
        out_shape=jax.ShapeDtypeStruct((bsz, seq, ATTN_WIDTH), BF16),
        scratch_shapes=[pltpu.VMEM((3, N_Q_HEADS, blk, 3 * blk), F32)],
        compiler_params=_params(("arbitrary", "arbitrary")),
        name="attn",
    )(*([p] * (1 + 2 * len(offs))), wl["bucket"], wl["rel_bias"], wl["attn_sink"])


def _outproj_kernel(lru_ref, rwkv_ref, attn_ref, x_ref, w_ref, g_ref, o_ref):
    mix = (_dot(lru_ref[...], w_ref[0:LRU_WIDTH, :])
           + _dot(rwkv_ref[...], w_ref[LRU_WIDTH:LRU_WIDTH + RWKV_WIDTH, :])
           + _dot(attn_ref[...], w_ref[LRU_WIDTH + RWKV_WIDTH:, :]))
    ms = jnp.mean(mix * mix, axis=-1, keepdims=True)
    o_ref[...] = x_ref[...] + mix * lax.rsqrt(ms + EPS) * g_ref[...]


def _outproj(lru, rwkv, attn, x2, w, gain):
    t = x2.shape[0]
    tm = min(TM_OUT, t)
    row = lambda width: pl.BlockSpec((tm, width), lambda i: (i, 0))
    return pl.pallas_call(
        _outproj_kernel,
        grid=(t // tm,),
        in_specs=[row(LRU_WIDTH), row(RWKV_WIDTH), row(ATTN_WIDTH), row(D_MODEL),
                  pl.BlockSpec((D_MODEL, D_MODEL), lambda i: (0, 0)),
                  pl.BlockSpec((1, D_MODEL), lambda i: (0, 0))],
        out_specs=row(D_MODEL),
        out_shape=jax.ShapeDtypeStruct((t, D_MODEL), F32),
        compiler_params=_params(("arbitrary",)),
        name="outproj",
    )(lru, rwkv, attn, x2, w, gain)


def _ffn_kernel(x_ref, gpre_ref, wu_ref, wd_ref, gpost_ref, o_ref, h_ref, acc_ref):
    f = pl.program_id(1)

    @pl.when(f == 0)
    def _():
        x = x_ref[...]
        ms = jnp.mean(x * x, axis=-1, keepdims=True)
        h_ref[...] = (x * lax.rsqrt(ms + EPS) * gpre_ref[...]).astype(BF16)
        acc_ref[...] = jnp.zeros_like(acc_ref)

    u = jnp.maximum(_dot(h_ref[...], wu_ref[...]), 0.0)
    acc_ref[...] += _dot((u * u).astype(BF16), wd_ref[...])

    @pl.when(f == pl.num_programs(1) - 1)
    def _():
        y = acc_ref[...]
        ms = jnp.mean(y * y, axis=-1, keepdims=True)
        o_ref[...] = x_ref[...] + y * lax.rsqrt(ms + EPS) * gpost_ref[...]


def _ffn(x2, gpre, wu, wd, gpost):
    t = x2.shape[0]
    tm = min(TM_FFN, t)
    return pl.pallas_call(
        _ffn_kernel,
        grid=(t // tm, D_FF // TF_FFN),
        in_specs=[pl.BlockSpec((tm, D_MODEL), lambda i, f: (i, 0)),
                  pl.BlockSpec((1, D_MODEL), lambda i, f: (0, 0)),
                  pl.BlockSpec((D_MODEL, TF_FFN), lambda i, f: (0, f)),
                  pl.BlockSpec((TF_FFN, D_MODEL), lambda i, f: (f, 0)),
                  pl.BlockSpec((1, D_MODEL), lambda i, f: (0, 0))],
        out_specs=pl.BlockSpec((tm, D_MODEL), lambda i, f: (i, 0)),
        out_shape=jax.ShapeDtypeStruct((t, D_MODEL), F32),
        scratch_shapes=[pltpu.VMEM((tm, D_MODEL), BF16), pltpu.VMEM((tm, D_MODEL), F32)],
        compiler_params=_params(("arbitrary", "arbitrary")),
        name="ffn",
    )(x2, gpre, wu, wd, gpost)


def _block_diag(w):
    n, bi, bj = w.shape
    eye = jnp.eye(n, dtype=w.dtype)
    return (eye[:, None, :, None] * w[:, :, None, :]).reshape(n * bi, n * bj)


def _pad_lora(w2, d):
    z = jnp.zeros_like(w2[d])
    return jnp.concatenate([w2[0], z] if d == 0 else [z, w2[1]], axis=0)


def _layer_weights(l, w):
    row = lambda v: v.reshape(1, -1).astype(F32)
    w_in = w["w_in"][l]
    w_in_p = jnp.concatenate(
        [w_in[:, OFF_RWKV:OFF_ATTN], jnp.zeros((D_MODEL, RWKV_PAD - RWKV_COLS), w_in.dtype),
         w_in[:, 0:OFF_RWKV], w_in[:, OFF_ATTN:OFF_ATTN + ATTN_WIDTH] * (HEAD_DIM ** -0.5),
         w_in[:, OFF_ATTN + ATTN_WIDTH:]], axis=1).astype(BF16)
    mu = jnp.concatenate([w["rwkv_mu"][l], jnp.zeros((RWKV_PAD - RWKV_COLS,), F32)]).reshape(1, -1)
    return {
        "norm_mix_pre": row(w["norm_mix_pre"][l]), "norm_mix_post": row(w["norm_mix_post"][l]),
        "norm_ffn_pre": row(w["norm_ffn_pre"][l]), "norm_ffn_post": row(w["norm_ffn_post"][l]),
        "w_in": w_in_p, "w_out": w["w_out"][l].astype(BF16),
        "conv_w": w["conv_w"][l].astype(F32), "conv_b": row(w["conv_b"][l]),
        "lru_wa": [_block_diag(w["lru_wa"][l, d]).astype(BF16) for d in range(2)],
        "lru_wx": [_block_diag(w["lru_wx"][l, d]).astype(BF16) for d in range(2)],
        "lru_ba": [row(w["lru_ba"][l, d]) for d in range(2)],
        "lru_bx": [row(w["lru_bx"][l, d]) for d in range(2)],
        "lru_lambda": [row(w["lru_lambda"][l, d]) for d in range(2)],
        "rwkv_mu": mu,
        "rwkv_w0": [row(w["rwkv_w0"][l, d]) for d in range(2)],
        "rwkv_w2": [_pad_lora(w["rwkv_w2"][l], d).astype(BF16) for d in range(2)],
        "rwkv_a0": [row(w["rwkv_a0"][l, d]) for d in range(2)],
        "rwkv_a2": [_pad_lora(w["rwkv_a2"][l], d).astype(BF16) for d in range(2)],
        "rwkv_g2": w["rwkv_g2"][l].astype(BF16),
        "rwkv_k_k": row(w["rwkv_k_k"][l]),
        "rwkv_k_a": [row(w["rwkv_k_a"][l, d]) for d in range(2)],
        "rwkv_r_k": row(w["rwkv_r_k"][l]),
        "rwkv_gn_w": row(w["rwkv_gn_w"][l]), "rwkv_gn_b": row(w["rwkv_gn_b"][l]),
        "attn_sink": w["attn_sink"][l].reshape(1, -1).astype(F32),
        "rel_bias": w["rel_bias"].astype(F32),
        "bucket": jnp.asarray(_t5_bucket_table()),
        "w_up": w["w_up"][l].astype(BF16), "w_down": w["w_down"][l].astype(BF16),
    }


def _trunk(x, layers):
    bsz, seq, _ = x.shape
    t = bsz * seq
    x2 = x.reshape(t, D_MODEL)
    for wl in layers:
        p = _inproj(x2, wl["norm_mix_pre"], wl["w_in"]).reshape(bsz, seq, P_COLS)
        hf = _lru(p, None, wl, 0, False)
        lru_out = _lru(p, hf, wl, 1, True)
        rwkv_fwd = _rwkv(p, None, wl, 0, False)
        rwkv_out = _rwkv(p, rwkv_fwd, wl, 1, True)
        attn_out = _attn(p, wl)
        x2 = _outproj(lru_out.reshape(t, LRU_WIDTH), rwkv_out.reshape(t, RWKV_WIDTH),
                      attn_out.reshape(t, ATTN_WIDTH), x2, wl["w_out"], wl["norm_mix_post"])
        x2 = _ffn(x2, wl["norm_ffn_pre"], wl["w_up"], wl["w_down"], wl["norm_ffn_post"])
    return x2.reshape(bsz, seq, D_MODEL)


def kernel(x_prompt, x_sample, norm_mix_pre, norm_mix_post, norm_ffn_pre, norm_ffn_post, w_in, w_out, conv_w, conv_b, lru_wa, lru_ba, lru_wx, lru_bx, lru_lambda, rwkv_mu, rwkv_w0, rwkv_w2, rwkv_a0, rwkv_a2, rwkv_g2, rwkv_k_k, rwkv_k_a, rwkv_r_k, rwkv_gn_w, rwkv_gn_b, attn_sink, rel_bias, w_up, w_down):
    w = dict(norm_mix_pre=norm_mix_pre, norm_mix_post=norm_mix_post, norm_ffn_pre=norm_ffn_pre,
             norm_ffn_post=norm_ffn_post, w_in=w_in, w_out=w_out, conv_w=conv_w, conv_b=conv_b,
             lru_wa=lru_wa, lru_ba=lru_ba, lru_wx=lru_wx, lru_bx=lru_bx, lru_lambda=lru_lambda,
             rwkv_mu=rwkv_mu, rwkv_w0=rwkv_w0, rwkv_w2=rwkv_w2, rwkv_a0=rwkv_a0, rwkv_a2=rwkv_a2,
             rwkv_g2=rwkv_g2, rwkv_k_k=rwkv_k_k, rwkv_k_a=rwkv_k_a, rwkv_r_k=rwkv_r_k,
             rwkv_gn_w=rwkv_gn_w, rwkv_gn_b=rwkv_gn_b, attn_sink=attn_sink, rel_bias=rel_bias,
             w_up=w_up, w_down=w_down)
    layers = [_layer_weights(l, w) for l in range(w_in.shape[0])]
    return (_trunk(x_prompt, layers), _trunk(x_sample, layers))
```

```python
import functools
import math

import numpy as np
import jax
import jax.numpy as jnp
from jax import lax
from jax.experimental import pallas as pl
from jax.experimental.pallas import tpu as pltpu

F32 = jnp.float32
BF16 = jnp.bfloat16

D_MODEL = 2048
LRU_WIDTH = 512
LRU_BLOCKS = 8
LRU_C = 8.0
RWKV_WIDTH = 512
RWKV_HEAD = 64
RWKV_HEADS = 8
LORA_RANK = 64
GATE_RANK = 128
DECAY_SCALE = 0.606531
GN_EPS = 64e-5
ATTN_WIDTH = 1024
HEAD_DIM = 128
N_Q_HEADS = 8
N_KV_HEADS = 2
GQA_GROUP = 4
KV_WIDTH = 256
WINDOW = 128
ATTN_BLOCK = 128
N_BUCKETS = 32
MAX_DISTANCE = 128
D_FF = 8192
EPS = 1e-6
OFF_RWKV = 2 * LRU_WIDTH
RWKV_COLS = 3 * RWKV_WIDTH + 4 * LORA_RANK + GATE_RANK
OFF_ATTN = OFF_RWKV + RWKV_COLS

RWKV_PAD = 2048
P_LRU_X = 2048
P_LRU_G = 2560
P_Q = 3072
P_K = 4096
P_V = 4352
P_COLS = 4608

LANES = 128
HALF = 256
VMEM_LIMIT = 56 * 1024 * 1024

TM_PROJ = 1024
TN_PROJ = 1536
TM_OUT = 512
TM_FFN = 512
TF_FFN = 1024
TB_LRU = 1024
ATTN_QBLOCKS = 4
TB_RWKV = 512
SUB_RWKV = 256
CHUNK = 64
NEG = -1e30


def _dot(a, b):
    return jnp.dot(a, b, preferred_element_type=F32)


def _dot_nt(a, b):
    return lax.dot_general(a, b, (((1,), (1,)), ((), ())), preferred_element_type=F32)


def _dot_tn(a, b):
    return lax.dot_general(a, b, (((0,), (0,)), ((), ())), preferred_element_type=F32)


def _split2(x):
    hi = x.astype(BF16)
    lo = (x - hi.astype(F32)).astype(BF16)
    return hi, lo


def _params(sem):
    return pltpu.CompilerParams(dimension_semantics=sem, vmem_limit_bytes=VMEM_LIMIT)


def _inproj_kernel(x_ref, g_ref, w_ref, o_ref, h_ref):
    @pl.when(pl.program_id(1) == 0)
    def _():
        x = x_ref[...]
        ms = jnp.mean(x * x, axis=-1, keepdims=True)
        h_ref[...] = (x * lax.rsqrt(ms + EPS) * g_ref[...]).astype(BF16)

    o_ref[...] = _dot(h_ref[...], w_ref[...])


def _inproj(x2, gain, w):
    t = x2.shape[0]
    tm = min(TM_PROJ, t)
    return pl.pallas_call(
        _inproj_kernel,
        grid=(t // tm, P_COLS // TN_PROJ),
        in_specs=[
            pl.BlockSpec((tm, D_MODEL), lambda i, j: (i, 0)),
            pl.BlockSpec((1, D_MODEL), lambda i, j: (0, 0)),
            pl.BlockSpec((D_MODEL, TN_PROJ), lambda i, j: (0, j)),
        ],
        out_specs=pl.BlockSpec((tm, TN_PROJ), lambda i, j: (i, j)),
        out_shape=jax.ShapeDtypeStruct((t, P_COLS), F32),
        scratch_shapes=[pltpu.VMEM((tm, D_MODEL), BF16)],
        compiler_params=_params(("arbitrary", "arbitrary")),
        name="inproj",
    )(x2, gain, w)


def _time_block(j, nb, reverse):
    return nb - 1 - j if reverse else j


def _halo_specs(tb, nb, width, col, reverse):
    rows8 = tb // 8
    last8 = nb * rows8 - 1

    def blk(b, j):
        return (b, _time_block(j, nb, reverse), col)

    def prev(b, j):
        return (b, jnp.maximum(_time_block(j, nb, reverse) * rows8 - 1, 0), col)

    def nxt(b, j):
        return (b, jnp.minimum((_time_block(j, nb, reverse) + 1) * rows8, last8), col)

    return [pl.BlockSpec((None, tb, width), blk),
            pl.BlockSpec((None, 8, width), prev),
            pl.BlockSpec((None, 8, width), nxt)]


def _shift_time(x3, halo, back):
    r8 = lax.broadcasted_iota(jnp.int32, (1, 8, 1), 1)
    shift = back % 8
    w = pltpu.roll(x3, shift, 1)
    wh = pltpu.roll(halo, shift, 0)[None]
    if back > 0:
        return jnp.where(r8 < back, jnp.concatenate([wh, w[:-1]], axis=0), w)
    return jnp.where(r8 >= 8 + back, jnp.concatenate([w[1:], wh], axis=0), w)


def _softplus(z):
    return jnp.maximum(z, 0.0) + jnp.log(1.0 + jnp.exp(-jnp.abs(z)))


def _lru_kernel(*refs, reverse, final, tb, nb, seq):
    if final:
        xc_ref, g_ref, hf_ref, wa_ref, wx_ref, ba_ref, bx_ref, lam_ref, o_ref, carry_ref = refs
    else:
        (x_ref, xp_ref, xn_ref, cw_ref, cb_ref, wa_ref, wx_ref, ba_ref, bx_ref, lam_ref,
         o_ref, xc_ref, carry_ref) = refs
    j = pl.program_id(1)
    jj = _time_block(j, nb, reverse)

    @pl.when(j == 0)
    def _():
        carry_ref[...] = jnp.zeros_like(carry_ref)

    ngroups = tb // 8
    width = xc_ref.shape[1]
    row = lax.broadcasted_iota(jnp.int32, (tb, 1), 0)
    if final:
        xc = xc_ref[...]
    else:
        x3 = x_ref[...].reshape(ngroups, 8, width)
        prev8 = jnp.where(jj > 0, xp_ref[...], 0.0)
        next8 = jnp.where(jj < nb - 1, xn_ref[...], 0.0)
        cw = cw_ref[...]
        xc = (cw[0:1] * _shift_time(x3, prev8, 2) + cw[1:2] * _shift_time(x3, prev8, 1) + cw[2:3] * x3
              + cw[3:4] * _shift_time(x3, next8, -1) + cb_ref[...]).reshape(tb, width)
        xc_ref[...] = xc

    xcb = xc.astype(BF16)
    r = jax.nn.sigmoid(_dot(xcb, wa_ref[...]) + ba_ref[...])
    i = jax.nn.sigmoid(_dot(xcb, wx_ref[...]) + bx_ref[...])
    a = jnp.exp2(r * ((-LRU_C * math.log2(math.e)) * _softplus(-lam_ref[...])))
    om = 1.0 - a * a
    mult = jnp.where(om > 0.0, om * lax.rsqrt(om), 0.0)
    gpos = jj * tb + row
    first = (gpos == seq - 1) if reverse else (gpos == 0)
    mult = jnp.where(first, 1.0, mult)
    u = mult * i * xc

    r8 = lax.broadcasted_iota(jnp.int32, (1, 8, 1), 1)
    acc_a, acc_u = a.reshape(ngroups, 8, width), u.reshape(ngroups, 8, width)
    for s in (1, 2, 4):
        shift, valid = (8 - s, r8 < 8 - s) if reverse else (s, r8 >= s)
        sh_a, sh_u = pltpu.roll(acc_a, shift, 1), pltpu.roll(acc_u, shift, 1)
        acc_u = acc_a * jnp.where(valid, sh_u, 0.0) + acc_u
        acc_a = acc_a * jnp.where(valid, sh_a, 1.0)
    acc_a, acc_u = acc_a.reshape(tb, width), acc_u.reshape(tb, width)
    last_row = 0 if reverse else 7
    carry = carry_ref[...]
    carries = [None] * ngroups
    for gi in (range(ngroups - 1, -1, -1) if reverse else range(ngroups)):
        carries[gi] = carry
        at = 8 * gi + last_row
        carry = (jnp.broadcast_to(acc_a[at:at + 1], carry.shape) * carry
                 + jnp.broadcast_to(acc_u[at:at + 1], carry.shape))
    carry_ref[...] = carry
    h = acc_a * jnp.concatenate(carries, axis=0) + acc_u

    if final:
        o_ref[...] = ((hf_ref[...] + h) * jax.nn.gelu(g_ref[...], approximate=True)).astype(o_ref.dtype)
    else:
        o_ref[...] = h


def _lru(p, fwd, wl, d, final):
    bsz, seq, _ = p.shape
    tb = min(TB_LRU, seq)
    nb = seq // tb
    reverse = d == 1
    blk = lambda b, j: (b, _time_block(j, nb, reverse), 0)
    rows = pl.BlockSpec((None, tb, LRU_WIDTH), blk)
    const = lambda shape: pl.BlockSpec(shape, lambda b, j: (0,) * len(shape))
    if final:
        hf, xc = fwd
        in_specs = [rows, pl.BlockSpec((None, tb, LRU_WIDTH),
                                       lambda b, j: (b, _time_block(j, nb, reverse), P_LRU_G // LRU_WIDTH)), rows]
        args = [xc, p, hf]
        out_specs = rows
        out_shape = jax.ShapeDtypeStruct((bsz, seq, LRU_WIDTH), BF16)
    else:
        in_specs = _halo_specs(tb, nb, LRU_WIDTH, P_LRU_X // LRU_WIDTH, reverse)
        in_specs += [const((4, LRU_WIDTH)), const((1, LRU_WIDTH))]
        args = [p, p, p, wl["conv_w"], wl["conv_b"]]
        out_specs = [rows, rows]
        out_shape = [jax.ShapeDtypeStruct((bsz, seq, LRU_WIDTH), F32)] * 2
    in_specs += [const((LRU_WIDTH, LRU_WIDTH)), const((LRU_WIDTH, LRU_WIDTH)), const((1, LRU_WIDTH)),
                 const((1, LRU_WIDTH)), const((1, LRU_WIDTH))]
    args += [wl["lru_wa"][d], wl["lru_wx"][d], wl["lru_ba"][d], wl["lru_bx"][d], wl["lru_lambda"][d]]
    return pl.pallas_call(
        functools.partial(_lru_kernel, reverse=reverse, final=final, tb=tb, nb=nb, seq=seq),
        grid=(bsz, nb),
        in_specs=in_specs,
        out_specs=out_specs,
        out_shape=out_shape,
        scratch_shapes=[pltpu.VMEM((8, LRU_WIDTH), F32)],
        compiler_params=_params(("arbitrary", "arbitrary")),
        name="lru_bwd" if final else "lru_fwd",
    )(*args)


def _seg_sum(x, ones_bd):
    xb = x.astype(BF16)
    return jnp.concatenate(
        [_dot(xb[:, HALF * hf:HALF * (hf + 1)], ones_bd) for hf in range(x.shape[1] // HALF)], axis=1)


def _zero_of(x):
    bits = pltpu.bitcast(x, jnp.uint32)
    return ((bits >> 16) >> 16).astype(F32)


def _rwkv_kernel(*refs, reverse, final, tb, nb, sub, chunk):
    if final:
        (zs_ref, kkn_ref, yf_ref, tri_ref, ones_ref, hm_ref, bdm_ref, stm_ref,
         w0_ref, w2_ref, a0_ref, a2_ref, ka_ref,
         a0f_ref, a2f_ref, kaf_ref, g2_ref, rk_ref, gnw_ref, gnb_ref,
         o_ref, s_ref, qk_s, rt_s, kt_s, bt_s, ktg_s, btg_s, v_s, rp_s, egl_s, y_s, en_s, f_s) = refs
    else:
        (z_ref, zp_ref, zn_ref, tri_ref, ones_ref, hm_ref, bdm_ref, stm_ref,
         mu_ref, w0_ref, w2_ref, a0_ref, a2_ref, kk_ref, ka_ref,
         o_ref, zs_ref, kkn_ref, s_ref, qk_s, rt_s, kt_s, bt_s, ktg_s, btg_s, v_s, rp_s, egl_s, y_s, en_s,
         f_s) = refs
    j = pl.program_id(1)
    jj = _time_block(j, nb, reverse)
    W = RWKV_WIDTH
    group = HALF // chunk
    halves = W // HALF
    nchunk = sub // chunk
    n_levels = int(math.log2(chunk))
    order = range(nchunk - 1, -1, -1) if reverse else range(nchunk)

    @pl.when(j == 0)
    def _():
        s_ref[...] = jnp.zeros_like(s_ref)

    ones_bd = ones_ref[...]
    tri = tri_ref[...]
    head_mask = hm_ref[...]
    bd_mask = bdm_ref[...]
    st_mask = stm_ref[...]
    wi = lax.broadcasted_iota(jnp.int32, (chunk, HALF), 0)
    wj = lax.broadcasted_iota(jnp.int32, (chunk, HALF), 1) % chunk
    strict = (wj > wi) if reverse else (wj < wi)
    incl = (wj >= wi) if reverse else (wj <= wi)
    eye_w = jnp.where(wi == wj, 1.0, 0.0)

    def stack(xb):
        return jnp.concatenate([xb] * group, axis=0) * head_mask

    def blockdiag(xw):
        return jnp.concatenate([xw.astype(BF16)] * group, axis=0) * bd_mask

    def lanes(hf):
        return slice(HALF * hf, HALF * hf + HALF)

    saved = {}
    tokens = []

    def prep(base):
        blk = slice(base, base + sub)
        if final:
            zs = zs_ref[blk, :]
        else:
            z3 = z_ref[blk, :].reshape(sub // 8, 8, RWKV_PAD)
            prev8 = jnp.where(jj > 0, zp_ref[...], 0.0) if base == 0 else z_ref[base - 8:base, :]
            next8 = (jnp.where(jj < nb - 1, zn_ref[...], 0.0) if base + sub == tb
                     else z_ref[base + sub:base + sub + 8, :])
            neigh = _shift_time(z3, prev8, 1) + _shift_time(z3, next8, -1)
            yield
            zs = (z3 + (0.5 * neigh - z3) * mu_ref[...]).reshape(sub, RWKV_PAD)
            zs_ref[blk, :] = zs
        yield
        r = zs[:, 0:W]
        k = zs[:, W:2 * W]
        v = zs[:, 2 * W:3 * W]
        dw = zs[:, 3 * W:3 * W + 2 * LORA_RANK]
        da = zs[:, 3 * W + 2 * LORA_RANK:3 * W + 4 * LORA_RANK]
        dg = zs[:, 3 * W + 4 * LORA_RANK:3 * W + 4 * LORA_RANK + GATE_RANK]

        lw = -DECAY_SCALE * jax.nn.sigmoid(_dot(jnp.tanh(dw).astype(BF16), w2_ref[...]) + w0_ref[...])
        yield
        dab = da.astype(BF16)
        a = jax.nn.sigmoid(_dot(dab, a2_ref[...]) + a0_ref[...])
        kdir = k * (1.0 + (a - 1.0) * ka_ref[...])
        yield
        if final:
            kk = kkn_ref[blk, :]
        else:
            kk = k * kk_ref[...]
            kk = kk * lax.rsqrt(jnp.maximum(_seg_sum(kk * kk, ones_bd), 1e-24))
            kkn_ref[blk, :] = kk
        b = kk * a
        yield

        hi, lo = _split2(lw)
        g = _dot(tri, hi) + _dot(tri, lo)
        ends = [(c if reverse else c + 1) * chunk - (0 if reverse else 1) for c in range(nchunk)]
        gl = jnp.concatenate([jnp.broadcast_to(g[e:e + 1], (chunk, W)) for e in ends], axis=0)
        yield
        e_g = jnp.exp(g)
        e_ng = jnp.exp(-g)
        e_gl = jnp.exp(gl)
        yield
        kt = kdir * e_ng
        bt = b * e_ng
        qk_s[blk, :] = (kk * jnp.exp(g - lw)).astype(BF16)
        rt_s[blk, :] = (r * e_g).astype(BF16)
        yield
        kt_s[blk, :] = kt.astype(BF16)
        bt_s[blk, :] = bt.astype(BF16)
        ktg_s[blk, :] = (kt * e_gl).astype(BF16)
        yield
        btg_s[blk, :] = (-(bt * e_gl)).astype(BF16)
        v_s[blk, :] = v.astype(BF16)
        egl_s[blk, :] = e_gl
        tokens.append(e_gl[0:8, 0:LANES])
        if final:
            yield
            a_f = jax.nn.sigmoid(_dot(dab, a2f_ref[...]) + a0f_ref[...])
            kdir_f = k * (1.0 + (a_f - 1.0) * kaf_ref[...])
            bonus = _seg_sum(r * (kdir_f + kdir) * rk_ref[...], ones_bd) * v
            yield
            gate = _dot(jax.nn.sigmoid(dg).astype(BF16), g2_ref[...])
            saved[base] = (bonus, gate)
            tokens.append(gate[0:8, 0:LANES] + bonus[0:8, 0:LANES])

    def chains(bases):
        ids = [(base // chunk + c, hf) for base in bases for c in order for hf in range(halves)]
        n = len(ids)

        def rows(c):
            return slice(c * chunk, (c + 1) * chunk)

        def slot(c, hf):
            return c * halves + hf

        a_k, b_k, b_b, pw, inv = [], [], [], [], []
        for c, hf in ids:
            sl, ls = rows(c), lanes(hf)
            x2 = jnp.concatenate([qk_s[sl, ls], rt_s[sl, ls]], axis=0)
            gk = _dot_nt(x2, stack(kt_s[sl, ls]))
            gb = _dot_nt(x2, stack(bt_s[sl, ls]))
            a_k.append(jnp.where(strict, gk[:chunk], 0.0).astype(BF16))
            b_k.append(jnp.where(incl, gk[chunk:], 0.0).astype(BF16))
            b_b.append(jnp.where(incl, gb[chunk:], 0.0).astype(BF16))
            inv.append(jnp.where(strict, -gb[:chunk], 0.0))
            yield
        for i in range(n):
            pw.append(_dot(inv[i].astype(BF16), blockdiag(inv[i])))
            inv[i] = eye_w + inv[i]
            yield
        for level in range(1, n_levels):
            for i in range(n):
                bdp = blockdiag(pw[i])
                if level < n_levels - 1:
                    both = _dot(jnp.concatenate([pw[i], inv[i]], axis=0).astype(BF16), bdp)
                    pw[i] = both[:chunk]
                    inv[i] = inv[i] + both[chunk:]
                else:
                    inv[i] = inv[i] + _dot(inv[i].astype(BF16), bdp)
                yield
        tw = [t.astype(BF16) for t in inv]
        vm, akv, qp, vp = [], [], [], []
        for i, (c, hf) in enumerate(ids):
            vm.append(stack(v_s[rows(c), lanes(hf)]))
            akv.append(_dot(a_k[i], vm[i]))
            yield
        for i, (c, hf) in enumerate(ids):
            qp.append(_dot(tw[i], stack(qk_s[rows(c), lanes(hf)])).astype(BF16))
            yield
        for i in range(n):
            vp.append(_dot(tw[i], stack(akv[i].astype(BF16))).astype(BF16))
            yield
        for i, (c, hf) in enumerate(ids):
            sl, ls = rows(c), lanes(hf)
            rp_s[sl, ls] = (rt_s[sl, ls].astype(F32) - _dot(b_b[i], stack(qp[i]))).astype(BF16)
            yield
        for i, (c, hf) in enumerate(ids):
            y_s[rows(c), lanes(hf)] = _dot(b_k[i], vm[i]) - _dot(b_b[i], stack(vp[i]))
            yield
        for i, (c, hf) in enumerate(ids):
            en = _dot_tn(qp[i], btg_s[rows(c), lanes(hf)])
            en_s[slot(c, hf)] = (en * st_mask).astype(BF16)
            yield
        for i, (c, hf) in enumerate(ids):
            sl, ls = rows(c), lanes(hf)
            fm = _dot_tn(jnp.concatenate([v_s[sl, ls], vp[i]], axis=0),
                         jnp.concatenate([ktg_s[sl, ls], btg_s[sl, ls]], axis=0))
            fm = fm * st_mask
            if i == n - 1:
                while tokens:
                    fm = fm + _zero_of(tokens.pop())[0:1, 0:1]
            f_s[slot(c, hf)] = fm
            yield

    def walk(base):
        for c in order:
            sl = slice(base + c * chunk, base + (c + 1) * chunk)
            for hf in range(halves):
                ls = lanes(hf)
                sidx = (base // chunk + c) * halves + hf
                st = s_ref[hf]
                stb = st.astype(BF16)
                y_s[sl, ls] = y_s[sl, ls] + _dot_nt(rp_s[sl, ls], stb)
                decay = egl_s[sl, ls][0:1]
                s_ref[hf] = st * decay + _dot(stb, en_s[sidx]) + f_s[sidx]
            yield

    def tail(base):
        blk = slice(base, base + sub)
        if not final:
            y = y_s[blk, :]
            o_ref[blk, :] = y
            tokens.append(y[0:8, 0:LANES])
            return
        bonus, gate = saved.pop(base)
        y = yf_ref[blk, :] + y_s[blk, :]
        inv_n = 1.0 / RWKV_HEAD
        mean = _seg_sum(y, ones_bd) * inv_n
        yield
        yc = y - mean
        var = _seg_sum(yc * yc, ones_bd) * inv_n
        yield
        yn = yc * lax.rsqrt(var + GN_EPS) * gnw_ref[...] + gnb_ref[...]
        out = (yn + bonus) * gate
        o_ref[blk, :] = out.astype(o_ref.dtype)
        tokens.append(out[sub - 8:sub, 0:LANES])

    def walk_then_tail(base):
        yield from walk(base)
        yield from tail(base)

    def drain(gen):
        for _ in gen:
            pass

    def weave(main, sides, stride):
        sides, finished = list(sides), object()
        for step, _ in enumerate(main):
            if step % stride == 0:
                sides = [g for g in sides if next(g, finished) is not finished]
        for g in sides:
            drain(g)

    nsub = tb // sub
    bases = [sb * sub for sb in (range(nsub - 1, -1, -1) if reverse else range(nsub))]
    drain(prep(bases[0]))
    tokens.clear()
    for idx, base in enumerate(bases):
        sides = []
        if idx > 0:
            sides.append(walk_then_tail(bases[idx - 1]))
        if idx + 1 < nsub:
            sides.append(prep(bases[idx + 1]))
        weave(chains([base]), sides, 1)
    drain(walk_then_tail(bases[-1]))


def _rwkv_masks(tb, chunk, reverse):
    t = np.arange(tb)
    same = (t[:, None] // chunk) == (t[None, :] // chunk)
    before = (t[None, :] >= t[:, None]) if reverse else (t[None, :] <= t[:, None])
    h = np.arange(HALF)

    def bd(row_block, col_block):
        return (h[:, None] // row_block) == (h[None, :] // col_block)

    bf = lambda m: m.astype(np.float32).astype(jnp.bfloat16)
    return [bf(same & before), bf(bd(RWKV_HEAD, RWKV_HEAD)), bf(bd(chunk, RWKV_HEAD)),
            bf(bd(chunk, chunk)), bd(RWKV_HEAD, RWKV_HEAD).astype(np.float32)]


def _rwkv(p, fwd, wl, d, final):
    bsz, seq, _ = p.shape
    tb = min(TB_RWKV, seq)
    nb = seq // tb
    reverse = d == 1
    W = RWKV_WIDTH
    blk = lambda b, j: (b, _time_block(j, nb, reverse), 0)
    const = lambda shape: pl.BlockSpec(shape, lambda b, j: (0,) * len(shape))
    rows = lambda width: pl.BlockSpec((None, tb, width), blk)
    if final:
        yf, zs, kkn = fwd
        in_specs = [rows(RWKV_PAD), rows(W), rows(W)]
        args = [zs, kkn, yf]
    else:
        in_specs = _halo_specs(tb, nb, RWKV_PAD, 0, reverse)
        args = [p, p, p]
    sub = min(SUB_RWKV, tb)
    masks = _rwkv_masks(sub, CHUNK, reverse)
    in_specs += [const(m.shape) for m in masks]
    args += [jnp.asarray(m) for m in masks]
    if final:
        in_specs += [const((1, W)), const((2 * LORA_RANK, W)), const((1, W)), const((2 * LORA_RANK, W)),
                     const((1, W))]
        args += [wl["rwkv_w0"][d], wl["rwkv_w2"][d], wl["rwkv_a0"][d], wl["rwkv_a2"][d], wl["rwkv_k_a"][d]]
        in_specs += [const((1, W)), const((2 * LORA_RANK, W)), const((1, W)), const((GATE_RANK, W)),
                     const((1, W)), const((1, W)), const((1, W))]
        args += [wl["rwkv_a0"][0], wl["rwkv_a2"][0], wl["rwkv_k_a"][0], wl["rwkv_g2"], wl["rwkv_r_k"],
                 wl["rwkv_gn_w"], wl["rwkv_gn_b"]]
        out_specs = rows(W)
        out_shape = jax.ShapeDtypeStruct((bsz, seq, W), BF16)
    else:
        in_specs += [const((1, RWKV_PAD)), const((1, W)), const((2 * LORA_RANK, W)), const((1, W)),
                     const((2 * LORA_RANK, W)), const((1, W)), const((1, W))]
        args += [wl["rwkv_mu"], wl["rwkv_w0"][d], wl["rwkv_w2"][d], wl["rwkv_a0"][d], wl["rwkv_a2"][d],
                 wl["rwkv_k_k"], wl["rwkv_k_a"][d]]
        out_specs = [rows(W), rows(RWKV_PAD), rows(W)]
        out_shape = [jax.ShapeDtypeStruct((bsz, seq, W), F32), jax.ShapeDtypeStruct((bsz, seq, RWKV_PAD), F32),
                     jax.ShapeDtypeStruct((bsz, seq, W), F32)]
    scratch = [pltpu.VMEM((W // HALF, HALF, HALF), F32)]
    scratch += [pltpu.VMEM((tb, W), BF16)] * 8
    scratch += [pltpu.VMEM((tb, W), F32)] * 2
    nhc = (tb // CHUNK) * (W // HALF)
    scratch += [pltpu.VMEM((nhc, HALF, HALF), BF16), pltpu.VMEM((nhc, HALF, HALF), F32)]
    return pl.pallas_call(
        functools.partial(_rwkv_kernel, reverse=reverse, final=final, tb=tb, nb=nb, sub=sub, chunk=CHUNK),
        grid=(bsz, nb),
        in_specs=in_specs,
        out_specs=out_specs,
        out_shape=out_shape,
        scratch_shapes=scratch,
        compiler_params=_params(("arbitrary", "arbitrary")),
        name="rwkv_bwd" if final else "rwkv_fwd",
    )(*args)


def _t5_bucket_table():
    qi = np.arange(ATTN_BLOCK)[:, None]
    kj = np.arange(3 * ATTN_BLOCK)[None, :]
    rel = kj - ATTN_BLOCK - qi
    half = N_BUCKETS // 2
    max_exact = half // 2
    ret = np.where(rel > 0, half, 0)
    n = np.abs(rel)
    nf = np.maximum(n, 1).astype(np.float32)
    large = max_exact + (np.log(nf / max_exact) / math.log(MAX_DISTANCE / max_exact)
                         * (half - max_exact)).astype(np.int32)
    large = np.minimum(large, half - 1)
    return (ret + np.where(n < max_exact, n, large)).astype(np.int32)


def _attn_kernel(*refs, nb):
    nkv = ATTN_QBLOCKS + 2
    q_ref, k_refs, v_refs = refs[0], refs[1:1 + nkv], refs[1 + nkv:1 + 2 * nkv]
    bkt_ref, rb_ref, sink_ref, o_ref, bias_ref = refs[1 + 2 * nkv:]
    b = pl.program_id(0)
    j = pl.program_id(1)
    blk = ATTN_BLOCK

    @pl.when((b == 0) & (j == 0))
    def _():
        bucket = bkt_ref[...]
        qi = lax.broadcasted_iota(jnp.int32, (blk, 3 * blk), 0)
        kj = lax.broadcasted_iota(jnp.int32, (blk, 3 * blk), 1)
        in_window = jnp.abs(kj - blk - qi) <= WINDOW
        for h in range(N_Q_HEADS):
            def body(t, acc, h=h):
                return jnp.where(bucket == t, rb_ref[t, h], acc)
            bias = lax.fori_loop(0, N_BUCKETS, body, jnp.zeros((blk, 3 * blk), F32))
            bias = jnp.where(in_window, bias, NEG)
            bias_ref[0, h] = bias
            bias_ref[1, h] = jnp.where(kj < blk, NEG, bias)
            bias_ref[2, h] = jnp.where(kj >= 2 * blk, NEG, bias)

    hrow = lax.broadcasted_iota(jnp.int32, (GQA_GROUP * blk, 1), 0) // blk
    kblocks = [r[...].astype(BF16) for r in k_refs]
    vblocks = [r[...].astype(BF16) for r in v_refs]
    for qb in range(ATTN_QBLOCKS):
        first = (j == 0) if qb == 0 else False
        last = (j == nb - 1) if qb == ATTN_QBLOCKS - 1 else False
        variant = jnp.where(first, 1, jnp.where(last, 2, 0))
        kcat = jnp.concatenate(kblocks[qb:qb + 3], axis=0)
        vcat = jnp.concatenate(vblocks[qb:qb + 3], axis=0)
        qrows = slice(blk * qb, blk * (qb + 1))
        for g in range(N_KV_HEADS):
            qs = jnp.concatenate(
                [q_ref[qrows, HEAD_DIM * (GQA_GROUP * g + hh):HEAD_DIM * (GQA_GROUP * g + hh + 1)]
                 for hh in range(GQA_GROUP)], axis=0).astype(BF16)
            kg = kcat[:, HEAD_DIM * g:HEAD_DIM * (g + 1)]
            vg = vcat[:, HEAD_DIM * g:HEAD_DIM * (g + 1)]
            bias = bias_ref[variant, GQA_GROUP * g:GQA_GROUP * (g + 1)].reshape(GQA_GROUP * blk, 3 * blk)
            s = _dot_nt(qs, kg) + bias
            sk = jnp.zeros((GQA_GROUP * blk, 1), F32)
            for hh in range(GQA_GROUP):
                sk = jnp.where(hrow == hh, sink_ref[0, GQA_GROUP * g + hh], sk)
            m = jnp.maximum(jnp.max(s, axis=-1, keepdims=True), sk)
            pexp = jnp.exp(s - m)
            den = jnp.sum(pexp, axis=-1, keepdims=True) + jnp.exp(sk - m)
            o = _dot(pexp.astype(BF16), vg) / den
            for hh in range(GQA_GROUP):
                h = GQA_GROUP * g + hh
                o_ref[qrows, HEAD_DIM * h:HEAD_DIM * (h + 1)] = o[blk * hh:blk * (hh + 1)].astype(o_ref.dtype)


def _attn(p, wl):
    bsz, seq, _ = p.shape
    blk = ATTN_BLOCK
    qrows = ATTN_QBLOCKS * blk
    nb = seq // qrows
    nkb = seq // blk
    assert nkb >= 2, "the first and the last block of a sequence use different bias variants"
    kcol, vcol = P_K // KV_WIDTH, P_V // KV_WIDTH

    def kv(col, off):
        return pl.BlockSpec((None, blk, KV_WIDTH),
                            lambda b, j: (b, jnp.clip(ATTN_QBLOCKS * j + off, 0, nkb - 1), col))

    offs = range(-1, ATTN_QBLOCKS + 1)
    smem = lambda: pl.BlockSpec(memory_space=pltpu.SMEM)
    return pl.pallas_call(
        functools.partial(_attn_kernel, nb=nb),
        grid=(bsz, nb),
        in_specs=[pl.BlockSpec((None, qrows, ATTN_WIDTH), lambda b, j: (b, j, P_Q // ATTN_WIDTH))]
                 + [kv(kcol, o) for o in offs] + [kv(vcol, o) for o in offs]
                 + [pl.BlockSpec((blk, 3 * blk), lambda b, j: (0, 0)), smem(), smem()],
        out_specs=pl.BlockSpec((None, qrows, ATTN_WIDTH), lambda b, j: (b, j, 0)),
        out_shape=jax.ShapeDtypeStruct((bsz, seq, ATTN_WIDTH), BF16),
        scratch_shapes=[pltpu.VMEM((3, N_Q_HEADS, blk, 3 * blk), F32)],
        compiler_params=_params(("arbitrary", "arbitrary")),
        name="attn",
    )(*([p] * (1 + 2 * len(offs))), wl["bucket"], wl["rel_bias"], wl["attn_sink"])


def _outproj_kernel(lru_ref, rwkv_ref, attn_ref, x_ref, w_ref, g_ref, o_ref):
    mix = (_dot(lru_ref[...], w_ref[0:LRU_WIDTH, :])
           + _dot(rwkv_ref[...], w_ref[LRU_WIDTH:LRU_WIDTH + RWKV_WIDTH, :])
           + _dot(attn_ref[...], w_ref[LRU_WIDTH + RWKV_WIDTH:, :]))
    ms = jnp.mean(mix * mix, axis=-1, keepdims=True)
    o_ref[...] = x_ref[...] + mix * lax.rsqrt(ms + EPS) * g_ref[...]


def _outproj(lru, rwkv, attn, x2, w, gain):
    t = x2.shape[0]
    tm = min(TM_OUT, t)
    row = lambda width: pl.BlockSpec((tm, width), lambda i: (i, 0))
    return pl.pallas_call(
        _outproj_kernel,
        grid=(t // tm,),
        in_specs=[row(LRU_WIDTH), row(RWKV_WIDTH), row(ATTN_WIDTH), row(D_MODEL),
                  pl.BlockSpec((D_MODEL, D_MODEL), lambda i: (0, 0)),
                  pl.BlockSpec((1, D_MODEL), lambda i: (0, 0))],
        out_specs=row(D_MODEL),
        out_shape=jax.ShapeDtypeStruct((t, D_MODEL), F32),
        compiler_params=_params(("arbitrary",)),
        name="outproj",
    )(lru, rwkv, attn, x2, w, gain)


def _ffn_kernel(x_ref, gpre_ref, wu_ref, wd_ref, gpost_ref, o_ref, h_ref, acc_ref):
    f = pl.program_id(1)

    @pl.when(f == 0)
    def _():
        x = x_ref[...]
        ms = jnp.mean(x * x, axis=-1, keepdims=True)
        h_ref[...] = (x * lax.rsqrt(ms + EPS) * gpre_ref[...]).astype(BF16)
        acc_ref[...] = jnp.zeros_like(acc_ref)

    u = jnp.maximum(_dot(h_ref[...], wu_ref[...]), 0.0)
    acc_ref[...] += _dot((u * u).astype(BF16), wd_ref[...])

    @pl.when(f == pl.num_programs(1) - 1)
    def _():
        y = acc_ref[...]
        ms = jnp.mean(y * y, axis=-1, keepdims=True)
        o_ref[...] = x_ref[...] + y * lax.rsqrt(ms + EPS) * gpost_ref[...]


def _ffn(x2, gpre, wu, wd, gpost):
    t = x2.shape[0]
    tm = min(TM_FFN, t)
    return pl.pallas_call(
        _ffn_kernel,
        grid=(t // tm, D_FF // TF_FFN),
        in_specs=[pl.BlockSpec((tm, D_MODEL), lambda i, f: (i, 0)),
                  pl.BlockSpec((1, D_MODEL), lambda i, f: (0, 0)),
                  pl.BlockSpec((D_MODEL, TF_FFN), lambda i, f: (0, f)),
                  pl.BlockSpec((TF_FFN, D_MODEL), lambda i, f: (f, 0)),
                  pl.BlockSpec((1, D_MODEL), lambda i, f: (0, 0))],
        out_specs=pl.BlockSpec((tm, D_MODEL), lambda i, f: (i, 0)),
        out_shape=jax.ShapeDtypeStruct((t, D_MODEL), F32),
        scratch_shapes=[pltpu.VMEM((tm, D_MODEL), BF16), pltpu.VMEM((tm, D_MODEL), F32)],
        compiler_params=_params(("arbitrary", "arbitrary")),
        name="ffn",
    )(x2, gpre, wu, wd, gpost)


def _block_diag(w):
    n, bi, bj = w.shape
    eye = jnp.eye(n, dtype=w.dtype)
    return (eye[:, None, :, None] * w[:, :, None, :]).reshape(n * bi, n * bj)


def _pad_lora(w2, d):
    z = jnp.zeros_like(w2[d])
    return jnp.concatenate([w2[0], z] if d == 0 else [z, w2[1]], axis=0)


def _layer_weights(l, w):
    row = lambda v: v.reshape(1, -1).astype(F32)
    w_in = w["w_in"][l]
    w_in_p = jnp.concatenate(
        [w_in[:, OFF_RWKV:OFF_ATTN], jnp.zeros((D_MODEL, RWKV_PAD - RWKV_COLS), w_in.dtype),
         w_in[:, 0:OFF_RWKV], w_in[:, OFF_ATTN:OFF_ATTN + ATTN_WIDTH] * (HEAD_DIM ** -0.5),
         w_in[:, OFF_ATTN + ATTN_WIDTH:]], axis=1).astype(BF16)
    mu = jnp.concatenate([w["rwkv_mu"][l], jnp.zeros((RWKV_PAD - RWKV_COLS,), F32)]).reshape(1, -1)
    return {
        "norm_mix_pre": row(w["norm_mix_pre"][l]), "norm_mix_post": row(w["norm_mix_post"][l]),
        "norm_ffn_pre": row(w["norm_ffn_pre"][l]), "norm_ffn_post": row(w["norm_ffn_post"][l]),
        "w_in": w_in_p, "w_out": w["w_out"][l].astype(BF16),
        "conv_w": w["conv_w"][l].astype(F32), "conv_b": row(w["conv_b"][l]),
        "lru_wa": [_block_diag(w["lru_wa"][l, d]).astype(BF16) for d in range(2)],
        "lru_wx": [_block_diag(w["lru_wx"][l, d]).astype(BF16) for d in range(2)],
        "lru_ba": [row(w["lru_ba"][l, d]) for d in range(2)],
        "lru_bx": [row(w["lru_bx"][l, d]) for d in range(2)],
        "lru_lambda": [row(w["lru_lambda"][l, d]) for d in range(2)],
        "rwkv_mu": mu,
        "rwkv_w0": [row(w["rwkv_w0"][l, d]) for d in range(2)],
        "rwkv_w2": [_pad_lora(w["rwkv_w2"][l], d).astype(BF16) for d in range(2)],
        "rwkv_a0": [row(w["rwkv_a0"][l, d]) for d in range(2)],
        "rwkv_a2": [_pad_lora(w["rwkv_a2"][l], d).astype(BF16) for d in range(2)],
        "rwkv_g2": w["rwkv_g2"][l].astype(BF16),
        "rwkv_k_k": row(w["rwkv_k_k"][l]),
        "rwkv_k_a": [row(w["rwkv_k_a"][l, d]) for d in range(2)],
        "rwkv_r_k": row(w["rwkv_r_k"][l]),
        "rwkv_gn_w": row(w["rwkv_gn_w"][l]), "rwkv_gn_b": row(w["rwkv_gn_b"][l]),
        "attn_sink": w["attn_sink"][l].reshape(1, -1).astype(F32),
        "rel_bias": w["rel_bias"].astype(F32),
        "bucket": jnp.asarray(_t5_bucket_table()),
        "w_up": w["w_up"][l].astype(BF16), "w_down": w["w_down"][l].astype(BF16),
    }


def _trunk(x, layers):
    bsz, seq, _ = x.shape
    t = bsz * seq
    x2 = x.reshape(t, D_MODEL)
    for wl in layers:
        p = _inproj(x2, wl["norm_mix_pre"], wl["w_in"]).reshape(bsz, seq, P_COLS)
        hf = _lru(p, None, wl, 0, False)
        lru_out = _lru(p, hf, wl, 1, True)
        rwkv_fwd = _rwkv(p, None, wl, 0, False)
        rwkv_out = _rwkv(p, rwkv_fwd, wl, 1, True)
        attn_out = _attn(p, wl)
        x2 = _outproj(lru_out.reshape(t, LRU_WIDTH), rwkv_out.reshape(t, RWKV_WIDTH),
                      attn_out.reshape(t, ATTN_WIDTH), x2, wl["w_out"], wl["norm_mix_post"])
        x2 = _ffn(x2, wl["norm_ffn_pre"], wl["w_up"], wl["w_down"], wl["norm_ffn_post"])
    return x2.reshape(bsz, seq, D_MODEL)


def kernel(x_prompt, x_sample, norm_mix_pre, norm_mix_post, norm_ffn_pre, norm_ffn_post, w_in, w_out, conv_w, conv_b, lru_wa, lru_ba, lru_wx, lru_bx, lru_lambda, rwkv_mu, rwkv_w0, rwkv_w2, rwkv_a0, rwkv_a2, rwkv_g2, rwkv_k_k, rwkv_k_a, rwkv_r_k, rwkv_gn_w, rwkv_gn_b, attn_sink, rel_bias, w_up, w_down):
    w = dict(norm_mix_pre=norm_mix_pre, norm_mix_post=norm_mix_post, norm_ffn_pre=norm_ffn_pre,
             norm_ffn_post=norm_ffn_post, w_in=w_in, w_out=w_out, conv_w=conv_w, conv_b=conv_b,
             lru_wa=lru_wa, lru_ba=lru_ba, lru_wx=lru_wx, lru_bx=lru_bx, lru_lambda=lru_lambda,
             rwkv_mu=rwkv_mu, rwkv_w0=rwkv_w0, rwkv_w2=rwkv_w2, rwkv_a0=rwkv_a0, rwkv_a2=rwkv_a2,
             rwkv_g2=rwkv_g2, rwkv_k_k=rwkv_k_k, rwkv_k_a=rwkv_k_a, rwkv_r_k=rwkv_r_k,
             rwkv_gn_w=rwkv_gn_w, rwkv_gn_b=rwkv_gn_b, attn_sink=attn_sink, rel_bias=rel_bias,
             w_up=w_up, w_down=w_down)
    layers = [_layer_weights(l, w) for l in range(w_in.shape[0])]
    return (_trunk(x_prompt, layers), _trunk(x_sample, layers))
```

```python
import functools
import math

import numpy as np
import jax
import jax.numpy as jnp
from jax import lax
from jax.experimental import pallas as pl
from jax.experimental.pallas import tpu as pltpu

F32 = jnp.float32
BF16 = jnp.bfloat16

D_MODEL = 2048
LRU_WIDTH = 512
LRU_BLOCKS = 8
LRU_C = 8.0
RWKV_WIDTH = 512
RWKV_HEAD = 64
RWKV_HEADS = 8
LORA_RANK = 64
GATE_RANK = 128
DECAY_SCALE = 0.606531
GN_EPS = 64e-5
ATTN_WIDTH = 1024
HEAD_DIM = 128
N_Q_HEADS = 8
N_KV_HEADS = 2
GQA_GROUP = 4
KV_WIDTH = 256
WINDOW = 128
ATTN_BLOCK = 128
N_BUCKETS = 32
MAX_DISTANCE = 128
D_FF = 8192
EPS = 1e-6
OFF_RWKV = 2 * LRU_WIDTH
RWKV_COLS = 3 * RWKV_WIDTH + 4 * LORA_RANK + GATE_RANK
OFF_ATTN = OFF_RWKV + RWKV_COLS

RWKV_PAD = 2048
P_LRU_X = 2048
P_LRU_G = 2560
P_Q = 3072
P_K = 4096
P_V = 4352
P_COLS = 4608

LANES = 128
HALF = 256
VMEM_LIMIT = 56 * 1024 * 1024

TM_PROJ = 1024
TN_PROJ = 1536
TM_OUT = 512
TM_FFN = 512
TF_FFN = 1024
TB_LRU = 1024
ATTN_QBLOCKS = 4
TB_RWKV = 512
TB_RWKV_BWD = 1024
SUB_RWKV = 256
CHUNK = 64
NEG = -1e30


def _dot(a, b):
    return jnp.dot(a, b, preferred_element_type=F32)


def _dot_nt(a, b):
    return lax.dot_general(a, b, (((1,), (1,)), ((), ())), preferred_element_type=F32)


def _dot_tn(a, b):
    return lax.dot_general(a, b, (((0,), (0,)), ((), ())), preferred_element_type=F32)


def _split2(x):
    hi = x.astype(BF16)
    lo = (x - hi.astype(F32)).astype(BF16)
    return hi, lo


def _params(sem):
    return pltpu.CompilerParams(dimension_semantics=sem, vmem_limit_bytes=VMEM_LIMIT)


def _inproj_kernel(x_ref, g_ref, w_ref, o_ref, h_ref):
    @pl.when(pl.program_id(1) == 0)
    def _():
        x = x_ref[...]
        ms = jnp.mean(x * x, axis=-1, keepdims=True)
        h_ref[...] = (x * lax.rsqrt(ms + EPS) * g_ref[...]).astype(BF16)

    o_ref[...] = _dot(h_ref[...], w_ref[...])


def _inproj(x2, gain, w):
    t = x2.shape[0]
    tm = min(TM_PROJ, t)
    return pl.pallas_call(
        _inproj_kernel,
        grid=(t // tm, P_COLS // TN_PROJ),
        in_specs=[
            pl.BlockSpec((tm, D_MODEL), lambda i, j: (i, 0)),
            pl.BlockSpec((1, D_MODEL), lambda i, j: (0, 0)),
            pl.BlockSpec((D_MODEL, TN_PROJ), lambda i, j: (0, j)),
        ],
        out_specs=pl.BlockSpec((tm, TN_PROJ), lambda i, j: (i, j)),
        out_shape=jax.ShapeDtypeStruct((t, P_COLS), F32),
        scratch_shapes=[pltpu.VMEM((tm, D_MODEL), BF16)],
        compiler_params=_params(("arbitrary", "arbitrary")),
        name="inproj",
    )(x2, gain, w)


def _time_block(j, nb, reverse):
    return nb - 1 - j if reverse else j


def _halo_specs(tb, nb, width, col, reverse):
    rows8 = tb // 8
    last8 = nb * rows8 - 1

    def blk(b, j):
        return (b, _time_block(j, nb, reverse), col)

    def prev(b, j):
        return (b, jnp.maximum(_time_block(j, nb, reverse) * rows8 - 1, 0), col)

    def nxt(b, j):
        return (b, jnp.minimum((_time_block(j, nb, reverse) + 1) * rows8, last8), col)

    return [pl.BlockSpec((None, tb, width), blk),
            pl.BlockSpec((None, 8, width), prev),
            pl.BlockSpec((None, 8, width), nxt)]


def _shift_time(x3, halo, back):
    r8 = lax.broadcasted_iota(jnp.int32, (1, 8, 1), 1)
    shift = back % 8
    w = pltpu.roll(x3, shift, 1)
    wh = pltpu.roll(halo, shift, 0)[None]
    if back > 0:
        return jnp.where(r8 < back, jnp.concatenate([wh, w[:-1]], axis=0), w)
    return jnp.where(r8 >= 8 + back, jnp.concatenate([w[1:], wh], axis=0), w)


def _softplus(z):
    return jnp.maximum(z, 0.0) + jnp.log(1.0 + jnp.exp(-jnp.abs(z)))


def _lru_kernel(*refs, reverse, final, tb, nb, seq):
    if final:
        xc_ref, g_ref, hf_ref, wa_ref, wx_ref, ba_ref, bx_ref, lam_ref, o_ref, carry_ref = refs
    else:
        (x_ref, xp_ref, xn_ref, cw_ref, cb_ref, wa_ref, wx_ref, ba_ref, bx_ref, lam_ref,
         o_ref, xc_ref, carry_ref) = refs
    j = pl.program_id(1)
    jj = _time_block(j, nb, reverse)

    @pl.when(j == 0)
    def _():
        carry_ref[...] = jnp.zeros_like(carry_ref)

    ngroups = tb // 8
    width = xc_ref.shape[1]
    row = lax.broadcasted_iota(jnp.int32, (tb, 1), 0)
    if final:
        xc = xc_ref[...]
    else:
        x3 = x_ref[...].reshape(ngroups, 8, width)
        prev8 = jnp.where(jj > 0, xp_ref[...], 0.0)
        next8 = jnp.where(jj < nb - 1, xn_ref[...], 0.0)
        cw = cw_ref[...]
        xc = (cw[0:1] * _shift_time(x3, prev8, 2) + cw[1:2] * _shift_time(x3, prev8, 1) + cw[2:3] * x3
              + cw[3:4] * _shift_time(x3, next8, -1) + cb_ref[...]).reshape(tb, width)
        xc_ref[...] = xc

    xcb = xc.astype(BF16)
    r = jax.nn.sigmoid(_dot(xcb, wa_ref[...]) + ba_ref[...])
    i = jax.nn.sigmoid(_dot(xcb, wx_ref[...]) + bx_ref[...])
    a = jnp.exp2(r * ((-LRU_C * math.log2(math.e)) * _softplus(-lam_ref[...])))
    om = 1.0 - a * a
    mult = jnp.where(om > 0.0, om * lax.rsqrt(om), 0.0)
    gpos = jj * tb + row
    first = (gpos == seq - 1) if reverse else (gpos == 0)
    mult = jnp.where(first, 1.0, mult)
    u = mult * i * xc

    r8 = lax.broadcasted_iota(jnp.int32, (1, 8, 1), 1)
    acc_a, acc_u = a.reshape(ngroups, 8, width), u.reshape(ngroups, 8, width)
    for s in (1, 2, 4):
        shift, valid = (8 - s, r8 < 8 - s) if reverse else (s, r8 >= s)
        sh_a, sh_u = pltpu.roll(acc_a, shift, 1), pltpu.roll(acc_u, shift, 1)
        acc_u = acc_a * jnp.where(valid, sh_u, 0.0) + acc_u
        acc_a = acc_a * jnp.where(valid, sh_a, 1.0)
    acc_a, acc_u = acc_a.reshape(tb, width), acc_u.reshape(tb, width)
    last_row = 0 if reverse else 7
    carry = carry_ref[...]
    carries = [None] * ngroups
    for gi in (range(ngroups - 1, -1, -1) if reverse else range(ngroups)):
        carries[gi] = carry
        at = 8 * gi + last_row
        carry = (jnp.broadcast_to(acc_a[at:at + 1], carry.shape) * carry
                 + jnp.broadcast_to(acc_u[at:at + 1], carry.shape))
    carry_ref[...] = carry
    h = acc_a * jnp.concatenate(carries, axis=0) + acc_u

    if final:
        o_ref[...] = ((hf_ref[...] + h) * jax.nn.gelu(g_ref[...], approximate=True)).astype(o_ref.dtype)
    else:
        o_ref[...] = h


def _lru(p, fwd, wl, d, final):
    bsz, seq, _ = p.shape
    tb = min(TB_LRU, seq)
    nb = seq // tb
    reverse = d == 1
    blk = lambda b, j: (b, _time_block(j, nb, reverse), 0)
    rows = pl.BlockSpec((None, tb, LRU_WIDTH), blk)
    const = lambda shape: pl.BlockSpec(shape, lambda b, j: (0,) * len(shape))
    if final:
        hf, xc = fwd
        in_specs = [rows, pl.BlockSpec((None, tb, LRU_WIDTH),
                                       lambda b, j: (b, _time_block(j, nb, reverse), P_LRU_G // LRU_WIDTH)), rows]
        args = [xc, p, hf]
        out_specs = rows
        out_shape = jax.ShapeDtypeStruct((bsz, seq, LRU_WIDTH), BF16)
    else:
        in_specs = _halo_specs(tb, nb, LRU_WIDTH, P_LRU_X // LRU_WIDTH, reverse)
        in_specs += [const((4, LRU_WIDTH)), const((1, LRU_WIDTH))]
        args = [p, p, p, wl["conv_w"], wl["conv_b"]]
        out_specs = [rows, rows]
        out_shape = [jax.ShapeDtypeStruct((bsz, seq, LRU_WIDTH), F32)] * 2
    in_specs += [const((LRU_WIDTH, LRU_WIDTH)), const((LRU_WIDTH, LRU_WIDTH)), const((1, LRU_WIDTH)),
                 const((1, LRU_WIDTH)), const((1, LRU_WIDTH))]
    args += [wl["lru_wa"][d], wl["lru_wx"][d], wl["lru_ba"][d], wl["lru_bx"][d], wl["lru_lambda"][d]]
    return pl.pallas_call(
        functools.partial(_lru_kernel, reverse=reverse, final=final, tb=tb, nb=nb, seq=seq),
        grid=(bsz, nb),
        in_specs=in_specs,
        out_specs=out_specs,
        out_shape=out_shape,
        scratch_shapes=[pltpu.VMEM((8, LRU_WIDTH), F32)],
        compiler_params=_params(("arbitrary", "arbitrary")),
        name="lru_bwd" if final else "lru_fwd",
    )(*args)


def _seg_sum(x, ones_bd):
    xb = x.astype(BF16)
    return jnp.concatenate(
        [_dot(xb[:, HALF * hf:HALF * (hf + 1)], ones_bd) for hf in range(x.shape[1] // HALF)], axis=1)


def _zero_of(x):
    bits = pltpu.bitcast(x, jnp.uint32)
    return ((bits >> 16) >> 16).astype(F32)


def _rwkv_kernel(*refs, reverse, final, tb, nb, sub, chunk):
    if final:
        (zs_ref, kkn_ref, yf_ref, tri_ref, ones_ref, hm_ref, bdm_ref, stm_ref,
         w0_ref, w2_ref, a0_ref, a2_ref, ka_ref,
         a0f_ref, a2f_ref, kaf_ref, g2_ref, rk_ref, gnw_ref, gnb_ref,
         o_ref, s_ref, qk_s, rt_s, kt_s, bt_s, ktg_s, btg_s, v_s, rp_s, egl_s, y_s, en_s, f_s) = refs
    else:
        (z_ref, zp_ref, zn_ref, tri_ref, ones_ref, hm_ref, bdm_ref, stm_ref,
         mu_ref, w0_ref, w2_ref, a0_ref, a2_ref, kk_ref, ka_ref,
         o_ref, zs_ref, kkn_ref, s_ref, qk_s, rt_s, kt_s, bt_s, ktg_s, btg_s, v_s, rp_s, egl_s, y_s, en_s,
         f_s) = refs
    j = pl.program_id(1)
    jj = _time_block(j, nb, reverse)
    W = RWKV_WIDTH
    group = HALF // chunk
    halves = W // HALF
    nchunk = sub // chunk
    n_levels = int(math.log2(chunk))
    order = range(nchunk - 1, -1, -1) if reverse else range(nchunk)

    @pl.when(j == 0)
    def _():
        s_ref[...] = jnp.zeros_like(s_ref)

    ones_bd = ones_ref[...]
    tri = tri_ref[...]
    head_mask = hm_ref[...]
    bd_mask = bdm_ref[...]
    st_mask = stm_ref[...]
    wi = lax.broadcasted_iota(jnp.int32, (chunk, HALF), 0)
    wj = lax.broadcasted_iota(jnp.int32, (chunk, HALF), 1) % chunk
    strict = (wj > wi) if reverse else (wj < wi)
    incl = (wj >= wi) if reverse else (wj <= wi)
    eye_w = jnp.where(wi == wj, 1.0, 0.0)

    def stack(xb):
        return jnp.concatenate([xb] * group, axis=0) * head_mask

    def blockdiag(xw):
        return jnp.concatenate([xw.astype(BF16)] * group, axis=0) * bd_mask

    def lanes(hf):
        return slice(HALF * hf, HALF * hf + HALF)

    saved = {}
    tokens = []

    def prep(base):
        blk = slice(base, base + sub)
        if final:
            zs = zs_ref[blk, :]
        else:
            z3 = z_ref[blk, :].reshape(sub // 8, 8, RWKV_PAD)
            prev8 = jnp.where(jj > 0, zp_ref[...], 0.0) if base == 0 else z_ref[base - 8:base, :]
            next8 = (jnp.where(jj < nb - 1, zn_ref[...], 0.0) if base + sub == tb
                     else z_ref[base + sub:base + sub + 8, :])
            neigh = _shift_time(z3, prev8, 1) + _shift_time(z3, next8, -1)
            yield
            zs = (z3 + (0.5 * neigh - z3) * mu_ref[...]).reshape(sub, RWKV_PAD)
            zs_ref[blk, :] = zs
        yield
        r = zs[:, 0:W]
        k = zs[:, W:2 * W]
        v = zs[:, 2 * W:3 * W]
        dw = zs[:, 3 * W:3 * W + 2 * LORA_RANK]
        da = zs[:, 3 * W + 2 * LORA_RANK:3 * W + 4 * LORA_RANK]
        dg = zs[:, 3 * W + 4 * LORA_RANK:3 * W + 4 * LORA_RANK + GATE_RANK]

        lw = -DECAY_SCALE * jax.nn.sigmoid(_dot(jnp.tanh(dw).astype(BF16), w2_ref[...]) + w0_ref[...])
        yield
        dab = da.astype(BF16)
        a = jax.nn.sigmoid(_dot(dab, a2_ref[...]) + a0_ref[...])
        kdir = k * (1.0 + (a - 1.0) * ka_ref[...])
        yield
        if final:
            kk = kkn_ref[blk, :]
        else:
            kk = k * kk_ref[...]
            kk = kk * lax.rsqrt(jnp.maximum(_seg_sum(kk * kk, ones_bd), 1e-24))
            kkn_ref[blk, :] = kk
        b = kk * a
        yield

        hi, lo = _split2(lw)
        g = _dot(tri, hi) + _dot(tri, lo)
        ends = [(c if reverse else c + 1) * chunk - (0 if reverse else 1) for c in range(nchunk)]
        gl = jnp.concatenate([jnp.broadcast_to(g[e:e + 1], (chunk, W)) for e in ends], axis=0)
        yield
        e_g = jnp.exp(g)
        e_ng = jnp.exp(-g)
        e_gl = jnp.exp(gl)
        yield
        kt = kdir * e_ng
        bt = b * e_ng
        qk_s[blk, :] = (kk * jnp.exp(g - lw)).astype(BF16)
        rt_s[blk, :] = (r * e_g).astype(BF16)
        yield
        kt_s[blk, :] = kt.astype(BF16)
        bt_s[blk, :] = bt.astype(BF16)
        ktg_s[blk, :] = (kt * e_gl).astype(BF16)
        yield
        btg_s[blk, :] = (-(bt * e_gl)).astype(BF16)
        v_s[blk, :] = v.astype(BF16)
        egl_s[blk, :] = e_gl
        tokens.append(e_gl[0:8, 0:LANES])
        if final:
            yield
            a_f = jax.nn.sigmoid(_dot(dab, a2f_ref[...]) + a0f_ref[...])
            kdir_f = k * (1.0 + (a_f - 1.0) * kaf_ref[...])
            bonus = _seg_sum(r * (kdir_f + kdir) * rk_ref[...], ones_bd) * v
            yield
            gate = _dot(jax.nn.sigmoid(dg).astype(BF16), g2_ref[...])
            saved[base] = (bonus, gate)
            tokens.append(gate[0:8, 0:LANES] + bonus[0:8, 0:LANES])

    def chains(bases):
        ids = [(base // chunk + c, hf) for base in bases for c in order for hf in range(halves)]
        n = len(ids)

        def rows(c):
            return slice(c * chunk, (c + 1) * chunk)

        def slot(c, hf):
            return c * halves + hf

        a_k, b_k, b_b, pw, inv = [], [], [], [], []
        for c, hf in ids:
            sl, ls = rows(c), lanes(hf)
            x2 = jnp.concatenate([qk_s[sl, ls], rt_s[sl, ls]], axis=0)
            gk = _dot_nt(x2, stack(kt_s[sl, ls]))
            gb = _dot_nt(x2, stack(bt_s[sl, ls]))
            a_k.append(jnp.where(strict, gk[:chunk], 0.0).astype(BF16))
            b_k.append(jnp.where(incl, gk[chunk:], 0.0).astype(BF16))
            b_b.append(jnp.where(incl, gb[chunk:], 0.0).astype(BF16))
            inv.append(jnp.where(strict, -gb[:chunk], 0.0))
            yield
        for i in range(n):
            pw.append(_dot(inv[i].astype(BF16), blockdiag(inv[i])))
            inv[i] = eye_w + inv[i]
            yield
        for level in range(1, n_levels):
            for i in range(n):
                bdp = blockdiag(pw[i])
                if level < n_levels - 1:
                    both = _dot(jnp.concatenate([pw[i], inv[i]], axis=0).astype(BF16), bdp)
                    pw[i] = both[:chunk]
                    inv[i] = inv[i] + both[chunk:]
                else:
                    inv[i] = inv[i] + _dot(inv[i].astype(BF16), bdp)
                yield
        tw = [t.astype(BF16) for t in inv]
        vm, akv, qp, vp = [], [], [], []
        for i, (c, hf) in enumerate(ids):
            vm.append(stack(v_s[rows(c), lanes(hf)]))
            akv.append(_dot(a_k[i], vm[i]))
            yield
        for i, (c, hf) in enumerate(ids):
            qp.append(_dot(tw[i], stack(qk_s[rows(c), lanes(hf)])).astype(BF16))
            yield
        for i in range(n):
            vp.append(_dot(tw[i], stack(akv[i].astype(BF16))).astype(BF16))
            yield
        for i, (c, hf) in enumerate(ids):
            sl, ls = rows(c), lanes(hf)
            rp_s[sl, ls] = (rt_s[sl, ls].astype(F32) - _dot(b_b[i], stack(qp[i]))).astype(BF16)
            yield
        for i, (c, hf) in enumerate(ids):
            y_s[rows(c), lanes(hf)] = _dot(b_k[i], vm[i]) - _dot(b_b[i], stack(vp[i]))
            yield
        for i, (c, hf) in enumerate(ids):
            en = _dot_tn(qp[i], btg_s[rows(c), lanes(hf)])
            en_s[slot(c, hf)] = (en * st_mask).astype(BF16)
            yield
        for i, (c, hf) in enumerate(ids):
            sl, ls = rows(c), lanes(hf)
            fm = _dot_tn(jnp.concatenate([v_s[sl, ls], vp[i]], axis=0),
                         jnp.concatenate([ktg_s[sl, ls], btg_s[sl, ls]], axis=0))
            fm = fm * st_mask
            if i == n - 1:
                while tokens:
                    fm = fm + _zero_of(tokens.pop())[0:1, 0:1]
            f_s[slot(c, hf)] = fm
            yield

    def walk(base):
        for c in order:
            sl = slice(base + c * chunk, base + (c + 1) * chunk)
            for hf in range(halves):
                ls = lanes(hf)
                sidx = (base // chunk + c) * halves + hf
                st = s_ref[hf]
                stb = st.astype(BF16)
                y_s[sl, ls] = y_s[sl, ls] + _dot_nt(rp_s[sl, ls], stb)
                decay = egl_s[sl, ls][0:1]
                s_ref[hf] = st * decay + _dot(stb, en_s[sidx]) + f_s[sidx]
            yield

    def tail(base):
        blk = slice(base, base + sub)
        if not final:
            y = y_s[blk, :]
            o_ref[blk, :] = y
            tokens.append(y[0:8, 0:LANES])
            return
        bonus, gate = saved.pop(base)
        y = yf_ref[blk, :] + y_s[blk, :]
        inv_n = 1.0 / RWKV_HEAD
        mean = _seg_sum(y, ones_bd) * inv_n
        yield
        yc = y - mean
        var = _seg_sum(yc * yc, ones_bd) * inv_n
        yield
        yn = yc * lax.rsqrt(var + GN_EPS) * gnw_ref[...] + gnb_ref[...]
        out = (yn + bonus) * gate
        o_ref[blk, :] = out.astype(o_ref.dtype)
        tokens.append(out[sub - 8:sub, 0:LANES])

    def walk_then_tail(base):
        yield from walk(base)
        yield from tail(base)

    def drain(gen):
        for _ in gen:
            pass

    def weave(main, sides, stride):
        sides, finished = list(sides), object()
        for step, _ in enumerate(main):
            if step % stride == 0:
                sides = [g for g in sides if next(g, finished) is not finished]
        for g in sides:
            drain(g)

    nsub = tb // sub
    bases = [sb * sub for sb in (range(nsub - 1, -1, -1) if reverse else range(nsub))]
    drain(prep(bases[0]))
    tokens.clear()
    for idx, base in enumerate(bases):
        sides = []
        if idx > 0:
            sides.append(walk_then_tail(bases[idx - 1]))
        if idx + 1 < nsub:
            sides.append(prep(bases[idx + 1]))
        weave(chains([base]), sides, 1)
    drain(walk_then_tail(bases[-1]))


def _rwkv_masks(tb, chunk, reverse):
    t = np.arange(tb)
    same = (t[:, None] // chunk) == (t[None, :] // chunk)
    before = (t[None, :] >= t[:, None]) if reverse else (t[None, :] <= t[:, None])
    h = np.arange(HALF)

    def bd(row_block, col_block):
        return (h[:, None] // row_block) == (h[None, :] // col_block)

    bf = lambda m: m.astype(np.float32).astype(jnp.bfloat16)
    return [bf(same & before), bf(bd(RWKV_HEAD, RWKV_HEAD)), bf(bd(chunk, RWKV_HEAD)),
            bf(bd(chunk, chunk)), bd(RWKV_HEAD, RWKV_HEAD).astype(np.float32)]


def _rwkv(p, fwd, wl, d, final):
    bsz, seq, _ = p.shape
    tb = min(TB_RWKV_BWD if final else TB_RWKV, seq)
    nb = seq // tb
    reverse = d == 1
    W = RWKV_WIDTH
    blk = lambda b, j: (b, _time_block(j, nb, reverse), 0)
    const = lambda shape: pl.BlockSpec(shape, lambda b, j: (0,) * len(shape))
    rows = lambda width: pl.BlockSpec((None, tb, width), blk)
    if final:
        yf, zs, kkn = fwd
        in_specs = [rows(RWKV_PAD), rows(W), rows(W)]
        args = [zs, kkn, yf]
    else:
        in_specs = _halo_specs(tb, nb, RWKV_PAD, 0, reverse)
        args = [p, p, p]
    sub = min(SUB_RWKV, tb)
    masks = _rwkv_masks(sub, CHUNK, reverse)
    in_specs += [const(m.shape) for m in masks]
    args += [jnp.asarray(m) for m in masks]
    if final:
        in_specs += [const((1, W)), const((2 * LORA_RANK, W)), const((1, W)), const((2 * LORA_RANK, W)),
                     const((1, W))]
        args += [wl["rwkv_w0"][d], wl["rwkv_w2"][d], wl["rwkv_a0"][d], wl["rwkv_a2"][d], wl["rwkv_k_a"][d]]
        in_specs += [const((1, W)), const((2 * LORA_RANK, W)), const((1, W)), const((GATE_RANK, W)),
                     const((1, W)), const((1, W)), const((1, W))]
        args += [wl["rwkv_a0"][0], wl["rwkv_a2"][0], wl["rwkv_k_a"][0], wl["rwkv_g2"], wl["rwkv_r_k"],
                 wl["rwkv_gn_w"], wl["rwkv_gn_b"]]
        out_specs = rows(W)
        out_shape = jax.ShapeDtypeStruct((bsz, seq, W), BF16)
    else:
        in_specs += [const((1, RWKV_PAD)), const((1, W)), const((2 * LORA_RANK, W)), const((1, W)),
                     const((2 * LORA_RANK, W)), const((1, W)), const((1, W))]
        args += [wl["rwkv_mu"], wl["rwkv_w0"][d], wl["rwkv_w2"][d], wl["rwkv_a0"][d], wl["rwkv_a2"][d],
                 wl["rwkv_k_k"], wl["rwkv_k_a"][d]]
        out_specs = [rows(W), rows(RWKV_PAD), rows(W)]
        out_shape = [jax.ShapeDtypeStruct((bsz, seq, W), F32), jax.ShapeDtypeStruct((bsz, seq, RWKV_PAD), F32),
                     jax.ShapeDtypeStruct((bsz, seq, W), F32)]
    scratch = [pltpu.VMEM((W // HALF, HALF, HALF), F32)]
    scratch += [pltpu.VMEM((tb, W), BF16)] * 8
    scratch += [pltpu.VMEM((tb, W), F32)] * 2
    nhc = (tb // CHUNK) * (W // HALF)
    scratch += [pltpu.VMEM((nhc, HALF, HALF), BF16), pltpu.VMEM((nhc, HALF, HALF), F32)]
    return pl.pallas_call(
        functools.partial(_rwkv_kernel, reverse=reverse, final=final, tb=tb, nb=nb, sub=sub, chunk=CHUNK),
        grid=(bsz, nb),
        in_specs=in_specs,
        out_specs=out_specs,
        out_shape=out_shape,
        scratch_shapes=scratch,
        compiler_params=_params(("arbitrary", "arbitrary")),
        name="rwkv_bwd" if final else "rwkv_fwd",
    )(*args)


def _t5_bucket_table():
    qi = np.arange(ATTN_BLOCK)[:, None]
    kj = np.arange(3 * ATTN_BLOCK)[None, :]
    rel = kj - ATTN_BLOCK - qi
    half = N_BUCKETS // 2
    max_exact = half // 2
    ret = np.where(rel > 0, half, 0)
    n = np.abs(rel)
    nf = np.maximum(n, 1).astype(np.float32)
    large = max_exact + (np.log(nf / max_exact) / math.log(MAX_DISTANCE / max_exact)
                         * (half - max_exact)).astype(np.int32)
    large = np.minimum(large, half - 1)
    return (ret + np.where(n < max_exact, n, large)).astype(np.int32)


def _attn_kernel(*refs, nb):
    nkv = ATTN_QBLOCKS + 2
    q_ref, k_refs, v_refs = refs[0], refs[1:1 + nkv], refs[1 + nkv:1 + 2 * nkv]
    bkt_ref, rb_ref, sink_ref, o_ref, bias_ref = refs[1 + 2 * nkv:]
    b = pl.program_id(0)
    j = pl.program_id(1)
    blk = ATTN_BLOCK

    @pl.when((b == 0) & (j == 0))
    def _():
        bucket = bkt_ref[...]
        qi = lax.broadcasted_iota(jnp.int32, (blk, 3 * blk), 0)
        kj = lax.broadcasted_iota(jnp.int32, (blk, 3 * blk), 1)
        in_window = jnp.abs(kj - blk - qi) <= WINDOW
        for h in range(N_Q_HEADS):
            def body(t, acc, h=h):
                return jnp.where(bucket == t, rb_ref[t, h], acc)
            bias = lax.fori_loop(0, N_BUCKETS, body, jnp.zeros((blk, 3 * blk), F32))
            bias = jnp.where(in_window, bias, NEG)
            bias_ref[0, h] = bias
            bias_ref[1, h] = jnp.where(kj < blk, NEG, bias)
            bias_ref[2, h] = jnp.where(kj >= 2 * blk, NEG, bias)

    hrow = lax.broadcasted_iota(jnp.int32, (GQA_GROUP * blk, 1), 0) // blk
    kblocks = [r[...].astype(BF16) for r in k_refs]
    vblocks = [r[...].astype(BF16) for r in v_refs]
    for qb in range(ATTN_QBLOCKS):
        first = (j == 0) if qb == 0 else False
        last = (j == nb - 1) if qb == ATTN_QBLOCKS - 1 else False
        variant = jnp.where(first, 1, jnp.where(last, 2, 0))
        kcat = jnp.concatenate(kblocks[qb:qb + 3], axis=0)
        vcat = jnp.concatenate(vblocks[qb:qb + 3], axis=0)
        qrows = slice(blk * qb, blk * (qb + 1))
        for g in range(N_KV_HEADS):
            qs = jnp.concatenate(
                [q_ref[qrows, HEAD_DIM * (GQA_GROUP * g + hh):HEAD_DIM * (GQA_GROUP * g + hh + 1)]
                 for hh in range(GQA_GROUP)], axis=0).astype(BF16)
            kg = kcat[:, HEAD_DIM * g:HEAD_DIM * (g + 1)]
            vg = vcat[:, HEAD_DIM * g:HEAD_DIM * (g + 1)]
            bias = bias_ref[variant, GQA_GROUP * g:GQA_GROUP * (g + 1)].reshape(GQA_GROUP * blk, 3 * blk)
            s = _dot_nt(qs, kg) + bias
            sk = jnp.zeros((GQA_GROUP * blk, 1), F32)
            for hh in range(GQA_GROUP):
                sk = jnp.where(hrow == hh, sink_ref[0, GQA_GROUP * g + hh], sk)
            m = jnp.maximum(jnp.max(s, axis=-1, keepdims=True), sk)
            pexp = jnp.exp(s - m)
            den = jnp.sum(pexp, axis=-1, keepdims=True) + jnp.exp(sk - m)
            o = _dot(pexp.astype(BF16), vg) / den
            for hh in range(GQA_GROUP):
                h = GQA_GROUP * g + hh
                o_ref[qrows, HEAD_DIM * h:HEAD_DIM * (h + 1)] = o[blk * hh:blk * (hh + 1)].astype(o_ref.dtype)


def _attn(p, wl):
    bsz, seq, _ = p.shape
    blk = ATTN_BLOCK
    qrows = ATTN_QBLOCKS * blk
    nb = seq // qrows
    nkb = seq // blk
    assert nkb >= 2, "the first and the last block of a sequence use different bias variants"
    kcol, vcol = P_K // KV_WIDTH, P_V // KV_WIDTH

    def kv(col, off):
        return pl.BlockSpec((None, blk, KV_WIDTH),
                            lambda b, j: (b, jnp.clip(ATTN_QBLOCKS * j + off, 0, nkb - 1), col))

    offs = range(-1, ATTN_QBLOCKS + 1)
    smem = lambda: pl.BlockSpec(memory_space=pltpu.SMEM)
    return pl.pallas_call(
        functools.partial(_attn_kernel, nb=nb),
        grid=(bsz, nb),
        in_specs=[pl.BlockSpec((None, qrows, ATTN_WIDTH), lambda b, j: (b, j, P_Q // ATTN_WIDTH))]
                 + [kv(kcol, o) for o in offs] + [kv(vcol, o) for o in offs]
                 + [pl.BlockSpec((blk, 3 * blk), lambda b, j: (0, 0)), smem(), smem()],
        out_specs=pl.BlockSpec((None, qrows, ATTN_WIDTH), lambda b, j: (b, j, 0)),
        out_shape=jax.ShapeDtypeStruct((bsz, seq, ATTN_WIDTH), BF16),
        scratch_shapes=[pltpu.VMEM((3, N_Q_HEADS, blk, 3 * blk), F32)],
        compiler_params=_params(("arbitrary", "arbitrary")),
        name="attn",
    )(*([p] * (1 + 2 * len(offs))), wl["bucket"], wl["rel_bias"], wl["attn_sink"])


def _outproj_kernel(lru_ref, rwkv_ref, attn_ref, x_ref, w_ref, g_ref, o_ref):
    mix = (_dot(lru_ref[...], w_ref[0:LRU_WIDTH, :])
           + _dot(rwkv_ref[...], w_ref[LRU_WIDTH:LRU_WIDTH + RWKV_WIDTH, :])
           + _dot(attn_ref[...], w_ref[LRU_WIDTH + RWKV_WIDTH:, :]))
    ms = jnp.mean(mix * mix, axis=-1, keepdims=True)
    o_ref[...] = x_ref[...] + mix * lax.rsqrt(ms + EPS) * g_ref[...]


def _outproj(lru, rwkv, attn, x2, w, gain):
    t = x2.shape[0]
    tm = min(TM_OUT, t)
    row = lambda width: pl.BlockSpec((tm, width), lambda i: (i, 0))
    return pl.pallas_call(
        _outproj_kernel,
        grid=(t // tm,),
        in_specs=[row(LRU_WIDTH), row(RWKV_WIDTH), row(ATTN_WIDTH), row(D_MODEL),
                  pl.BlockSpec((D_MODEL, D_MODEL), lambda i: (0, 0)),
                  pl.BlockSpec((1, D_MODEL), lambda i: (0, 0))],
        out_specs=row(D_MODEL),
        out_shape=jax.ShapeDtypeStruct((t, D_MODEL), F32),
        compiler_params=_params(("arbitrary",)),
        name="outproj",
    )(lru, rwkv, attn, x2, w, gain)


def _ffn_kernel(x_ref, gpre_ref, wu_ref, wd_ref, gpost_ref, o_ref, h_ref, acc_ref):
    f = pl.program_id(1)

    @pl.when(f == 0)
    def _():
        x = x_ref[...]
        ms = jnp.mean(x * x, axis=-1, keepdims=True)
        h_ref[...] = (x * lax.rsqrt(ms + EPS) * gpre_ref[...]).astype(BF16)
        acc_ref[...] = jnp.zeros_like(acc_ref)

    u = jnp.maximum(_dot(h_ref[...], wu_ref[...]), 0.0)
    acc_ref[...] += _dot((u * u).astype(BF16), wd_ref[...])

    @pl.when(f == pl.num_programs(1) - 1)
    def _():
        y = acc_ref[...]
        ms = jnp.mean(y * y, axis=-1, keepdims=True)
        o_ref[...] = x_ref[...] + y * lax.rsqrt(ms + EPS) * gpost_ref[...]


def _ffn(x2, gpre, wu, wd, gpost):
    t = x2.shape[0]
    tm = min(TM_FFN, t)
    return pl.pallas_call(
        _ffn_kernel,
        grid=(t // tm, D_FF // TF_FFN),
        in_specs=[pl.BlockSpec((tm, D_MODEL), lambda i, f: (i, 0)),
                  pl.BlockSpec((1, D_MODEL), lambda i, f: (0, 0)),
                  pl.BlockSpec((D_MODEL, TF_FFN), lambda i, f: (0, f)),
                  pl.BlockSpec((TF_FFN, D_MODEL), lambda i, f: (f, 0)),
                  pl.BlockSpec((1, D_MODEL), lambda i, f: (0, 0))],
        out_specs=pl.BlockSpec((tm, D_MODEL), lambda i, f: (i, 0)),
        out_shape=jax.ShapeDtypeStruct((t, D_MODEL), F32),
        scratch_shapes=[pltpu.VMEM((tm, D_MODEL), BF16), pltpu.VMEM((tm, D_MODEL), F32)],
        compiler_params=_params(("arbitrary", "arbitrary")),
        name="ffn",
    )(x2, gpre, wu, wd, gpost)


def _block_diag(w):
    n, bi, bj = w.shape
    eye = jnp.eye(n, dtype=w.dtype)
    return (eye[:, None, :, None] * w[:, :, None, :]).reshape(n * bi, n * bj)


def _pad_lora(w2, d):
    z = jnp.zeros_like(w2[d])
    return jnp.concatenate([w2[0], z] if d == 0 else [z, w2[1]], axis=0)


def _layer_weights(l, w):
    row = lambda v: v.reshape(1, -1).astype(F32)
    w_in = w["w_in"][l]
    w_in_p = jnp.concatenate(
        [w_in[:, OFF_RWKV:OFF_ATTN], jnp.zeros((D_MODEL, RWKV_PAD - RWKV_COLS), w_in.dtype),
         w_in[:, 0:OFF_RWKV], w_in[:, OFF_ATTN:OFF_ATTN + ATTN_WIDTH] * (HEAD_DIM ** -0.5),
         w_in[:, OFF_ATTN + ATTN_WIDTH:]], axis=1).astype(BF16)
    mu = jnp.concatenate([w["rwkv_mu"][l], jnp.zeros((RWKV_PAD - RWKV_COLS,), F32)]).reshape(1, -1)
    return {
        "norm_mix_pre": row(w["norm_mix_pre"][l]), "norm_mix_post": row(w["norm_mix_post"][l]),
        "norm_ffn_pre": row(w["norm_ffn_pre"][l]), "norm_ffn_post": row(w["norm_ffn_post"][l]),
        "w_in": w_in_p, "w_out": w["w_out"][l].astype(BF16),
        "conv_w": w["conv_w"][l].astype(F32), "conv_b": row(w["conv_b"][l]),
        "lru_wa": [_block_diag(w["lru_wa"][l, d]).astype(BF16) for d in range(2)],
        "lru_wx": [_block_diag(w["lru_wx"][l, d]).astype(BF16) for d in range(2)],
        "lru_ba": [row(w["lru_ba"][l, d]) for d in range(2)],
        "lru_bx": [row(w["lru_bx"][l, d]) for d in range(2)],
        "lru_lambda": [row(w["lru_lambda"][l, d]) for d in range(2)],
        "rwkv_mu": mu,
        "rwkv_w0": [row(w["rwkv_w0"][l, d]) for d in range(2)],
        "rwkv_w2": [_pad_lora(w["rwkv_w2"][l], d).astype(BF16) for d in range(2)],
        "rwkv_a0": [row(w["rwkv_a0"][l, d]) for d in range(2)],
        "rwkv_a2": [_pad_lora(w["rwkv_a2"][l], d).astype(BF16) for d in range(2)],
        "rwkv_g2": w["rwkv_g2"][l].astype(BF16),
        "rwkv_k_k": row(w["rwkv_k_k"][l]),
        "rwkv_k_a": [row(w["rwkv_k_a"][l, d]) for d in range(2)],
        "rwkv_r_k": row(w["rwkv_r_k"][l]),
        "rwkv_gn_w": row(w["rwkv_gn_w"][l]), "rwkv_gn_b": row(w["rwkv_gn_b"][l]),
        "attn_sink": w["attn_sink"][l].reshape(1, -1).astype(F32),
        "rel_bias": w["rel_bias"].astype(F32),
        "bucket": jnp.asarray(_t5_bucket_table()),
        "w_up": w["w_up"][l].astype(BF16), "w_down": w["w_down"][l].astype(BF16),
    }


def _trunk(x, layers):
    bsz, seq, _ = x.shape
    t = bsz * seq
    x2 = x.reshape(t, D_MODEL)
    for wl in layers:
        p = _inproj(x2, wl["norm_mix_pre"], wl["w_in"]).reshape(bsz, seq, P_COLS)
        hf = _lru(p, None, wl, 0, False)
        lru_out = _lru(p, hf, wl, 1, True)
        rwkv_fwd = _rwkv(p, None, wl, 0, False)
        rwkv_out = _rwkv(p, rwkv_fwd, wl, 1, True)
        attn_out = _attn(p, wl)
        x2 = _outproj(lru_out.reshape(t, LRU_WIDTH), rwkv_out.reshape(t, RWKV_WIDTH),
                      attn_out.reshape(t, ATTN_WIDTH), x2, wl["w_out"], wl["norm_mix_post"])
        x2 = _ffn(x2, wl["norm_ffn_pre"], wl["w_up"], wl["w_down"], wl["norm_ffn_post"])
    return x2.reshape(bsz, seq, D_MODEL)


def kernel(x_prompt, x_sample, norm_mix_pre, norm_mix_post, norm_ffn_pre, norm_ffn_post, w_in, w_out, conv_w, conv_b, lru_wa, lru_ba, lru_wx, lru_bx, lru_lambda, rwkv_mu, rwkv_w0, rwkv_w2, rwkv_a0, rwkv_a2, rwkv_g2, rwkv_k_k, rwkv_k_a, rwkv_r_k, rwkv_gn_w, rwkv_gn_b, attn_sink, rel_bias, w_up, w_down):
    w = dict(norm_mix_pre=norm_mix_pre, norm_mix_post=norm_mix_post, norm_ffn_pre=norm_ffn_pre,
             norm_ffn_post=norm_ffn_post, w_in=w_in, w_out=w_out, conv_w=conv_w, conv_b=conv_b,
             lru_wa=lru_wa, lru_ba=lru_ba, lru_wx=lru_wx, lru_bx=lru_bx, lru_lambda=lru_lambda,
             rwkv_mu=rwkv_mu, rwkv_w0=rwkv_w0, rwkv_w2=rwkv_w2, rwkv_a0=rwkv_a0, rwkv_a2=rwkv_a2,
             rwkv_g2=rwkv_g2, rwkv_k_k=rwkv_k_k, rwkv_k_a=rwkv_k_a, rwkv_r_k=rwkv_r_k,
             rwkv_gn_w=rwkv_gn_w, rwkv_gn_b=rwkv_gn_b, attn_sink=attn_sink, rel_bias=rel_bias,
             w_up=w_up, w_down=w_down)
    layers = [_layer_weights(l, w) for l in range(w_in.shape[0])]
    return (_trunk(x_prompt, layers), _trunk(x_sample, layers))
```
